```python
import jax, jax.numpy as jnp
from jax import lax
import numpy as np

D_MODEL = 1024
BATCH = 8
SEQ = 4096
DEPTH = 4

D_MIX = D_MODEL
D_LRU = D_MIX // 2
D_ATTN = D_MIX - D_LRU
LRU_BLOCKS = 8
LRU_BLOCK_DIM = D_LRU // LRU_BLOCKS
CONV_WIDTH = 4
LRU_C = 8.0
HEAD_DIM = 64
N_HEADS = D_ATTN // HEAD_DIM
DILATED_CONFIGS = ((128, 1), (512, 4), (2048, 16))
ATTN_BLOCK = 128
D_IN = 2 * D_LRU + 3 * D_ATTN
N_GROUPS = 4
EXPERTS_PER_GROUP = 8
N_EXPERTS = N_GROUPS * EXPERTS_PER_GROUP
TOP_K = 2
D_EXPERT = D_MODEL // 2
MOE_BLOCK = 128
EPS = 1e-6

kernel_name = "hymba_style_rglru_dilated_attn_hmoe"


def rms_norm(x, g):
    xf = x.astype(jnp.float32)
    y = xf * lax.rsqrt(jnp.mean(xf * xf, axis=-1, keepdims=True) + EPS)
    return (y * g.astype(jnp.float32)).astype(x.dtype)


def causal_depthwise_conv(x, w, b):
    k_width = w.shape[0]
    s_len = x.shape[1]
    xp = jnp.pad(x, ((0, 0), (k_width - 1, 0), (0, 0)))
    out = b
    for j in range(k_width):
        out = out + w[j] * xp[:, j:j + s_len]
    return out


def _lin_rec_combine(c1, c2):
    a1, b1 = c1
    a2, b2 = c2
    return a1 * a2, a2 * b1 + b2


def rg_lru(x, w_a, b_a, w_x, b_x, lam):
    xh = x.reshape(x.shape[0], x.shape[1], LRU_BLOCKS, LRU_BLOCK_DIM)
    r = jax.nn.sigmoid(jnp.einsum('bshi,hij->bshj', xh, w_a).reshape(x.shape) + b_a)
    i = jax.nn.sigmoid(jnp.einsum('bshi,hij->bshj', xh, w_x).reshape(x.shape) + b_x)
    log_a = -LRU_C * r * jax.nn.softplus(-lam)
    a = jnp.exp(log_a)
    u = jnp.sqrt(-jnp.expm1(2.0 * log_a)) * (i * x)
    _, h = lax.associative_scan(_lin_rec_combine, (a, u), axis=1)
    return h


def _dilated_branch(q, k, v, window, dilation):
    b, h, sp, dh = q.shape
    steps = window // dilation
    n_sub = sp // dilation
    nb = n_sub // ATTN_BLOCK

    def to_blocks(t):
        t = t.reshape(b, h, n_sub, dilation, dh).transpose(0, 1, 3, 2, 4)
        return t.reshape(b, h, dilation, nb, ATTN_BLOCK, dh)

    def with_prev(t):
        prev = jnp.pad(t, ((0, 0), (0, 0), (0, 0), (1, 0), (0, 0), (0, 0)))[:, :, :, :-1]
        return jnp.concatenate([prev, t], axis=4)

    qb = to_blocks(q)
    kc = with_prev(to_blocks(k))
    vc = with_prev(to_blocks(v))
    s = jnp.einsum('bhrnqd,bhrnkd->bhrnqk', qb, kc) * (HEAD_DIM ** -0.5)
    qi = jnp.arange(ATTN_BLOCK)[:, None]
    kj = jnp.arange(2 * ATTN_BLOCK)[None, :]
    dist = ATTN_BLOCK + qi - kj
    band = (dist >= 0) & (dist <= steps)
    first = (jnp.arange(nb) == 0)[:, None, None]
    valid = band[None] & ~(first & (kj < ATTN_BLOCK)[None])
    s = jnp.where(valid, s, -jnp.inf)
    m = jnp.max(s, axis=-1, keepdims=True)
    p = jnp.exp(s - m)
    denom = jnp.sum(p, axis=-1, keepdims=True)
    o = jnp.einsum('bhrnqk,bhrnkd->bhrnqd', p, vc) / denom
    lse = (m + jnp.log(denom))[..., 0]
    o = o.reshape(b, h, dilation, n_sub, dh).transpose(0, 1, 3, 2, 4).reshape(b, h, sp, dh)
    lse = lse.reshape(b, h, dilation, n_sub).transpose(0, 1, 3, 2).reshape(b, h, sp)
    return o, lse


def dilated_attention(q, k, v):
    s_len = q.shape[1]
    seg = ATTN_BLOCK * max(d for _, d in DILATED_CONFIGS)
    sp = -(-s_len // seg) * seg

    def prep(t):
        t = t.astype(jnp.float32).transpose(0, 2, 1, 3)
        return jnp.pad(t, ((0, 0), (0, 0), (0, sp - s_len), (0, 0)))

    qp, kp, vp = prep(q), prep(k), prep(v)
    outs = []
    lses = []
    for window, dilation in DILATED_CONFIGS:
        o, lse = _dilated_branch(qp, kp, vp, window, dilation)
        outs.append(o)
        lses.append(lse)
    wts = jax.nn.softmax(jnp.stack(lses, axis=0), axis=0)
    o = jnp.sum(jnp.stack(outs, axis=0) * wts[..., None], axis=0)
    return o[:, :, :s_len].transpose(0, 2, 1, 3)


def hierarchical_moe(h2, wg, bg, we, be, w_gate, w_up, w_down):
    n_tok, d = h2.shape
    hf = h2.astype(jnp.float32)
    group_prob = jax.nn.softmax(hf @ wg.astype(jnp.float32) + bg.astype(jnp.float32), axis=-1)
    p_top, g_idx = lax.top_k(group_prob, 1)
    expert_logits = (hf @ we.astype(jnp.float32) + be.astype(jnp.float32)).reshape(
        n_tok, N_GROUPS, EXPERTS_PER_GROUP)
    gi = jnp.broadcast_to(g_idx[:, :, None], (n_tok, 1, EXPERTS_PER_GROUP))
    local_logits = jnp.take_along_axis(expert_logits, gi, axis=1)[:, 0]
    top_vals, top_local = lax.top_k(local_logits, TOP_K)
    gate_w = jax.nn.softmax(top_vals, axis=-1) * p_top
    expert_idx = g_idx * EXPERTS_PER_GROUP + top_local

    n_asg = n_tok * TOP_K
    e_flat = expert_idx.reshape(n_asg)
    tok_flat = jnp.repeat(jnp.arange(n_tok, dtype=jnp.int32), TOP_K)
    g_flat = gate_w.reshape(n_asg)
    order = jnp.argsort(e_flat)
    e_s, tok_s, g_s = e_flat[order], tok_flat[order], g_flat[order]
    counts = jnp.bincount(e_flat, length=N_EXPERTS)
    padded = ((counts + MOE_BLOCK - 1) // MOE_BLOCK) * MOE_BLOCK
    start = jnp.cumsum(counts) - counts
    pend = jnp.cumsum(padded)
    pstart = pend - padded
    dest = pstart[e_s] + (jnp.arange(n_asg, dtype=jnp.int32) - start[e_s])
    n_rows = n_asg + N_EXPERTS * MOE_BLOCK
    n_blk = n_rows // MOE_BLOCK
    x_buf = jnp.zeros((n_rows, d), h2.dtype).at[dest].set(h2[tok_s])
    blk_expert = jnp.minimum(
        jnp.searchsorted(pend, jnp.arange(n_blk, dtype=jnp.int32) * MOE_BLOCK, side='right'),
        N_EXPERTS - 1)

    def expert_block(args):
        xb, e = args
        hid = jax.nn.silu(xb @ w_gate[e]) * (xb @ w_up[e])
        return hid @ w_down[e]

    y_buf = lax.map(expert_block, (x_buf.reshape(n_blk, MOE_BLOCK, d), blk_expert))
    y_buf = y_buf.reshape(n_rows, d)
    contrib = g_s[:, None].astype(y_buf.dtype) * y_buf[dest]
    return jnp.zeros((n_tok, d), h2.dtype).at[tok_s].add(contrib.astype(h2.dtype))


def setup_inputs(seed: int = 0) -> dict:
    key = jax.random.key(seed)
    ks = jax.random.split(key, 23)
    f32 = jnp.float32
    L = DEPTH

    def nrm(k, shape, scale):
        return jax.random.normal(k, shape, f32) * scale

    def gain(k, shape):
        return 1.0 + 0.05 * jax.random.normal(k, shape, f32)

    res_scale = (2.0 * DEPTH) ** -0.5
    u = jax.random.uniform(ks[9], (L, D_LRU), f32, 0.9, 0.999)
    a0 = u ** (1.0 / LRU_C)
    lam = jnp.log(a0) - jnp.log1p(-a0)
    return {
        "x": jax.random.normal(ks[0], (BATCH, SEQ, D_MODEL), f32),
        "norm_mix": gain(ks[1], (L, D_MODEL)),
        "w_in": nrm(ks[2], (L, D_MODEL, D_IN), D_MODEL ** -0.5),
        "conv_w": nrm(ks[3], (L, CONV_WIDTH, D_LRU), CONV_WIDTH ** -0.5),
        "conv_b": nrm(ks[4], (L, D_LRU), 0.02),
        "lru_w_a": nrm(ks[5], (L, LRU_BLOCKS, LRU_BLOCK_DIM, LRU_BLOCK_DIM), LRU_BLOCK_DIM ** -0.5),
        "lru_b_a": nrm(ks[6], (L, D_LRU), 0.02),
        "lru_w_x": nrm(ks[7], (L, LRU_BLOCKS, LRU_BLOCK_DIM, LRU_BLOCK_DIM), LRU_BLOCK_DIM ** -0.5),
        "lru_b_x": nrm(ks[8], (L, D_LRU), 0.02),
        "lru_lambda": lam,
        "q_norm": gain(ks[10], (L, HEAD_DIM)),
        "k_norm": gain(ks[11], (L, HEAD_DIM)),
        "norm_out_lru": gain(ks[12], (L, D_LRU)),
        "norm_out_attn": gain(ks[13], (L, D_ATTN)),
        "w_out": nrm(ks[14], (L, D_MIX, D_MODEL), D_MIX ** -0.5 * res_scale),
        "norm_ffn": gain(ks[15], (L, D_MODEL)),
        "router_group_w": nrm(ks[16], (L, D_MODEL, N_GROUPS), D_MODEL ** -0.5),
        "router_group_b": nrm(ks[17], (L, N_GROUPS), 0.01),
        "router_expert_w": nrm(ks[18], (L, D_MODEL, N_EXPERTS), D_MODEL ** -0.5),
        "router_expert_b": nrm(ks[19], (L, N_EXPERTS), 0.01),
        "w_gate": nrm(ks[20], (L, N_EXPERTS, D_MODEL, D_EXPERT), D_MODEL ** -0.5),
        "w_up": nrm(ks[21], (L, N_EXPERTS, D_MODEL, D_EXPERT), D_MODEL ** -0.5),
        "w_down": nrm(ks[22], (L, N_EXPERTS, D_EXPERT, D_MODEL), D_EXPERT ** -0.5 * res_scale),
    }


def reference(x, norm_mix, w_in, conv_w, conv_b, lru_w_a, lru_b_a, lru_w_x, lru_b_x,
              lru_lambda, q_norm, k_norm, norm_out_lru, norm_out_attn, w_out, norm_ffn,
              router_group_w, router_group_b, router_expert_w, router_expert_b,
              w_gate, w_up, w_down):
    b, s_len, d = x.shape
    o1 = D_LRU
    o2 = 2 * D_LRU
    o3 = o2 + D_ATTN
    o4 = o3 + D_ATTN
    for l in range(DEPTH):
        h = rms_norm(x, norm_mix[l])
        proj = jnp.einsum('bsd,de->bse', h, w_in[l])
        x_lru = proj[..., :o1]
        gate_lru = proj[..., o1:o2]
        q = proj[..., o2:o3].reshape(b, s_len, N_HEADS, HEAD_DIM)
        k = proj[..., o3:o4].reshape(b, s_len, N_HEADS, HEAD_DIM)
        v = proj[..., o4:].reshape(b, s_len, N_HEADS, HEAD_DIM)

        xc = causal_depthwise_conv(x_lru, conv_w[l], conv_b[l]).astype(jnp.float32)
        hl = rg_lru(xc, lru_w_a[l], lru_b_a[l], lru_w_x[l], lru_b_x[l], lru_lambda[l])
        y_lru = (hl * jax.nn.gelu(gate_lru.astype(jnp.float32))).astype(x.dtype)

        q = rms_norm(q, q_norm[l])
        k = rms_norm(k, k_norm[l])
        y_attn = dilated_attention(q, k, v).reshape(b, s_len, D_ATTN).astype(x.dtype)

        merged = jnp.concatenate(
            [rms_norm(y_lru, norm_out_lru[l]), rms_norm(y_attn, norm_out_attn[l])], axis=-1)
        x = x + jnp.einsum('bse,ed->bsd', merged, w_out[l])

        h2 = rms_norm(x, norm_ffn[l]).reshape(b * s_len, d)
        y = hierarchical_moe(h2, router_group_w[l], router_group_b[l], router_expert_w[l],
                             router_expert_b[l], w_gate[l], w_up[l], w_down[l])
        x = x + y.reshape(b, s_len, d)
    return x
```

```python
import functools

import jax
import jax.numpy as jnp
from jax import lax
from jax.experimental import pallas as pl
from jax.experimental.pallas import tpu as pltpu

F32 = jnp.float32
BF16 = jnp.bfloat16

EPS = 1e-6
LANES = 128
HEAD_DIM = 64
HEADS_PER_SLAB = LANES // HEAD_DIM
ATTN_BLOCK = 128
DILATIONS = (1, 4, 16)
CONV_WIDTH = 4
LRU_C = 8.0
N_GROUPS = 4
EXPERTS_PER_GROUP = 8
N_EXPERTS = N_GROUPS * EXPERTS_PER_GROUP
TOP_K = 2
ROUTER_LANES = LANES
EXPERT_LANE0 = N_GROUPS
VMEM_LIMIT = 48 * 1024 * 1024

TM_PROJ = 512
LRU_CHUNK = 256
NORM_CHUNK = 512
ROUTE_TILE = 512
MOE_BLOCK = 256
DMA_TILE = 256


def _rms(x, g):
    return x * lax.rsqrt(jnp.mean(x * x, axis=-1, keepdims=True) + EPS) * g


def _log1p(z):
    w = 1.0 + z
    return jnp.where(w == 1.0, z, z * jnp.log(w) / jnp.where(w == 1.0, 1.0, w - 1.0))


def _softplus(x):
    return jnp.maximum(x, 0.0) + _log1p(jnp.exp(-jnp.abs(x)))


def _one_minus_exp(x, exp_x):
    u = exp_x
    near = (1.0 - u) * x / jnp.log(jnp.where(u == 1.0, 0.5, u))
    near = jnp.where(u == 1.0, -x, near)
    return jnp.where(x > -0.5, near, 1.0 - u)


def _cparams(*sem):
    return pltpu.CompilerParams(dimension_semantics=sem, vmem_limit_bytes=VMEM_LIMIT)


def _in_proj_kernel(x_ref, g_ref, w_ref, xl_ref, gate_ref, q_ref, k_ref, v_ref, *, d_lru, d_attn):
    h = _rms(x_ref[0], g_ref[...]).astype(BF16)

    def cols(lo, n):
        return jnp.dot(h, w_ref[:, lo:lo + n], preferred_element_type=F32)

    xl_ref[0] = cols(0, d_lru)
    gate_ref[0] = cols(d_lru, d_lru)
    base = 2 * d_lru
    for ref in (q_ref, k_ref, v_ref):
        for j in range(d_attn // LANES):
            ref[0, j] = cols(base + j * LANES, LANES)
        base += d_attn


def _in_proj(x, g, w, d_lru, d_attn):
    b, s, d = x.shape
    tm = min(TM_PROJ, s)
    n_slab = d_attn // LANES
    slab = jax.ShapeDtypeStruct((b, n_slab, s, LANES), F32)
    row = jax.ShapeDtypeStruct((b, s, d_lru), F32)
    slab_spec = pl.BlockSpec((1, n_slab, tm, LANES), lambda i, j: (i, 0, j, 0))
    row_spec = pl.BlockSpec((1, tm, d_lru), lambda i, j: (i, j, 0))
    return pl.pallas_call(
        functools.partial(_in_proj_kernel, d_lru=d_lru, d_attn=d_attn),
        out_shape=(row, row, slab, slab, slab),
        grid=(b, s // tm),
        in_specs=[pl.BlockSpec((1, tm, d), lambda i, j: (i, j, 0)),
                  pl.BlockSpec((1, d), lambda i, j: (0, 0)),
                  pl.BlockSpec(w.shape, lambda i, j: (0, 0))],
        out_specs=(row_spec, row_spec, slab_spec, slab_spec, slab_spec),
        compiler_params=_cparams("parallel", "parallel"),
        name="in_proj",
    )(x, g, w)


def _lru_kernel(xl_ref, gate_ref, cw_ref, cb_ref, wa_ref, ba_ref, wx_ref, bx_ref, lam_ref,
                y_ref, xcat, h_scr):
    tc = xl_ref.shape[1]
    halo = 8

    @pl.when(pl.program_id(1) == 0)
    def _():
        xcat[0:halo, :] = jnp.zeros((halo, xcat.shape[1]), F32)
        h_scr[...] = jnp.zeros_like(h_scr)

    x = xl_ref[0]
    xcat[halo:halo + tc, :] = x
    xc = cb_ref[...]
    for j in range(CONV_WIDTH):
        xc = xc + cw_ref[j:j + 1, :] * xcat[pl.ds(halo - (CONV_WIDTH - 1) + j, tc), :]
    xcat[0:halo, :] = x[tc - halo:tc, :]

    xb = xc.astype(BF16)
    r = jax.nn.sigmoid(jnp.dot(xb, wa_ref[...], preferred_element_type=F32) + ba_ref[...])
    i = jax.nn.sigmoid(jnp.dot(xb, wx_ref[...], preferred_element_type=F32) + bx_ref[...])
    log_a = -LRU_C * r * _softplus(-lam_ref[...])
    a = jnp.exp(log_a)
    u = jnp.sqrt(_one_minus_exp(2.0 * log_a, a * a)) * (i * xc)

    row = lax.broadcasted_iota(jnp.int32, (tc, 1), 0)
    shift = 1
    while shift < tc:
        keep = row >= shift
        a_prev = pltpu.roll(a, shift, axis=0)
        u_prev = pltpu.roll(u, shift, axis=0)
        u = jnp.where(keep, a * u_prev + u, u)
        a = jnp.where(keep, a * a_prev, a)
        shift *= 2
    h = a * h_scr[...] + u
    h_scr[...] = h[tc - 1:tc, :]
    y_ref[0] = h * jax.nn.gelu(gate_ref[0])


def _lru(xl, gate, cw, cb, wa, ba, wx, bx, lam):
    b, s, c = xl.shape
    tc = min(LRU_CHUNK, s)
    row_spec = pl.BlockSpec((1, tc, c), lambda i, j: (i, j, 0))
    vec = pl.BlockSpec((1, c), lambda i, j: (0, 0))
    mat = pl.BlockSpec((c, c), lambda i, j: (0, 0))
    return pl.pallas_call(
        _lru_kernel,
        out_shape=jax.ShapeDtypeStruct((b, s, c), F32),
        grid=(b, s // tc),
        in_specs=[row_spec, row_spec, pl.BlockSpec((CONV_WIDTH, c), lambda i, j: (0, 0)), vec,
                  mat, vec, mat, vec, vec],
        out_specs=row_spec,
        scratch_shapes=[pltpu.VMEM((tc + 8, c), F32), pltpu.VMEM((1, c), F32)],
        compiler_params=_cparams("parallel", "arbitrary"),
        name="rg_lru",
    )(xl, gate, cw, cb, wa, ba, wx, bx, lam)


def _attn_kernel(q_ref, k_ref, v_ref, qg_ref, kg_ref, o_ref, qn, kn, m_s, l_s, acc_s):
    s_len = q_ref.shape[2]
    blk = ATTN_BLOCK
    lane = lax.broadcasted_iota(jnp.int32, (1, LANES), 1)
    lo = lane < HEAD_DIM
    n_chunk = s_len // NORM_CHUNK

    def norm_body(c, carry):
        rows = pl.ds(pl.multiple_of(c * NORM_CHUNK, NORM_CHUNK), NORM_CHUNK)
        for src, g_ref, dst, scale in ((q_ref, qg_ref, qn, HEAD_DIM ** -0.5), (k_ref, kg_ref, kn, 1.0)):
            x = src[0, 0, rows, :]
            x2 = x * x
            s0 = jnp.sum(jnp.where(lo, x2, 0.0), axis=-1, keepdims=True)
            s1 = jnp.sum(jnp.where(lo, 0.0, x2), axis=-1, keepdims=True)
            ms = jnp.where(lo, s0, s1) * (1.0 / HEAD_DIM)
            dst[rows, :] = x * lax.rsqrt(ms + EPS) * g_ref[...] * scale
        return carry

    lax.fori_loop(0, n_chunk, norm_body, 0)

    rows2 = lax.broadcasted_iota(jnp.int32, (HEADS_PER_SLAB * blk, blk), 0)
    qi = rows2 & (blk - 1)
    kj = lax.broadcasted_iota(jnp.int32, (HEADS_PER_SLAB * blk, blk), 1)
    band_prev = kj >= qi
    band_cur = kj <= qi
    nt = (((1,), (1,)), ((), ()))

    def branch(d, first_branch):
        log_d = d.bit_length() - 1
        span = blk * d

        def sl(start):
            return pl.ds(start, blk) if d == 1 else pl.ds(start, blk, stride=d)

        def body(it, carry):
            n = it >> log_d
            r = it & (d - 1)
            cur = sl(n * span + r)
            prev = sl(jnp.maximum(n - 1, 0) * span + r)
            qt = qn[cur, :]
            qs = jnp.concatenate([jnp.where(lo, qt, 0.0), jnp.where(lo, 0.0, qt)], axis=0).astype(BF16)
            s_c = lax.dot_general(qs, kn[cur, :].astype(BF16), nt, preferred_element_type=F32)
            s_p = lax.dot_general(qs, kn[prev, :].astype(BF16), nt, preferred_element_type=F32)
            s_c = jnp.where(band_cur, s_c, -jnp.inf)
            s_p = jnp.where(band_prev & (n > 0), s_p, -jnp.inf)
            m = jnp.maximum(jnp.max(s_c, axis=-1, keepdims=True), jnp.max(s_p, axis=-1, keepdims=True))
            p_c = jnp.exp(s_c - m)
            p_p = jnp.exp(s_p - m)
            l = jnp.sum(p_c, axis=-1, keepdims=True) + jnp.sum(p_p, axis=-1, keepdims=True)
            acc = (jnp.dot(p_c.astype(BF16), v_ref[0, 0, cur, :].astype(BF16), preferred_element_type=F32)
                   + jnp.dot(p_p.astype(BF16), v_ref[0, 0, prev, :].astype(BF16), preferred_element_type=F32))
            acc_t = jnp.where(lo, acc[:blk], acc[blk:])
            m_t = jnp.where(lo, jnp.broadcast_to(m[:blk], (blk, LANES)), jnp.broadcast_to(m[blk:], (blk, LANES)))
            l_t = jnp.where(lo, jnp.broadcast_to(l[:blk], (blk, LANES)), jnp.broadcast_to(l[blk:], (blk, LANES)))
            if first_branch:
                m_s[cur, :] = m_t
                l_s[cur, :] = l_t
                acc_s[cur, :] = acc_t
            else:
                m_o = m_s[cur, :]
                m_n = jnp.maximum(m_o, m_t)
                c_o = jnp.exp(m_o - m_n)
                c_t = jnp.exp(m_t - m_n)
                m_s[cur, :] = m_n
                l_s[cur, :] = l_s[cur, :] * c_o + l_t * c_t
                acc_s[cur, :] = acc_s[cur, :] * c_o + acc_t * c_t
            return carry

        lax.fori_loop(0, s_len // blk, body, 0)

    for idx, d in enumerate(DILATIONS):
        branch(d, idx == 0)

    def out_body(c, carry):
        rows = pl.ds(pl.multiple_of(c * NORM_CHUNK, NORM_CHUNK), NORM_CHUNK)
        o_ref[0, 0, rows, :] = acc_s[rows, :] / l_s[rows, :]
        return carry

    lax.fori_loop(0, n_chunk, out_body, 0)


def _attention(q, k, v, qg, kg):
    b, n_slab, s, _ = q.shape
    assert s % (ATTN_BLOCK * max(DILATIONS)) == 0 and s % NORM_CHUNK == 0
    slab_spec = pl.BlockSpec((1, 1, s, LANES), lambda i, j: (i, j, 0, 0))
    vec = pl.BlockSpec((1, LANES), lambda i, j: (0, 0))
    return pl.pallas_call(
        _attn_kernel,
        out_shape=jax.ShapeDtypeStruct(q.shape, F32),
        grid=(b, n_slab),
        in_specs=[slab_spec, slab_spec, slab_spec, vec, vec],
        out_specs=slab_spec,
        scratch_shapes=[pltpu.VMEM((s, LANES), F32) for _ in range(5)],
        compiler_params=_cparams("parallel", "parallel"),
        name="dilated_attn",
    )(q, k, v, qg, kg)


def _out_proj_kernel(x_ref, yl_ref, ya_ref, gl_ref, ga_ref, w_ref, gf_ref, wr_ref, br_ref,
                     xn_ref, h2_ref, lg_ref):
    d_lru = yl_ref.shape[2]
    nl = _rms(yl_ref[0], gl_ref[...]).astype(BF16)
    ya = jnp.concatenate([ya_ref[0, j] for j in range(ya_ref.shape[1])], axis=-1)
    na = _rms(ya, ga_ref[...]).astype(BF16)
    xn = x_ref[0] + (jnp.dot(nl, w_ref[0:d_lru, :], preferred_element_type=F32)
                     + jnp.dot(na, w_ref[d_lru:, :], preferred_element_type=F32))
    xn_ref[0] = xn
    h2 = _rms(xn, gf_ref[...])
    h2_ref[0] = h2
    lg_ref[0] = jnp.dot(h2, wr_ref[...], preferred_element_type=F32,
                        precision=lax.Precision.HIGHEST) + br_ref[...]


def _out_proj(x, yl, ya, gl, ga, w, gf, wr, br):
    b, s, d = x.shape
    d_lru = yl.shape[2]
    n_slab = ya.shape[1]
    tm = min(TM_PROJ, s)
    tok = pl.BlockSpec((1, tm, d), lambda i, j: (i, j, 0))
    const = lambda shape: pl.BlockSpec(shape, lambda i, j: (0,) * len(shape))
    return pl.pallas_call(
        _out_proj_kernel,
        out_shape=(jax.ShapeDtypeStruct((b, s, d), F32), jax.ShapeDtypeStruct((b, s, d), F32),
                   jax.ShapeDtypeStruct((b, s, ROUTER_LANES), F32)),
        grid=(b, s // tm),
        in_specs=[tok, pl.BlockSpec((1, tm, d_lru), lambda i, j: (i, j, 0)),
                  pl.BlockSpec((1, n_slab, tm, LANES), lambda i, j: (i, 0, j, 0)),
                  const((1, d_lru)), const((1, n_slab * LANES)), const(w.shape), const((1, d)),
                  const(wr.shape), const((1, ROUTER_LANES))],
        out_specs=(tok, tok, pl.BlockSpec((1, tm, ROUTER_LANES), lambda i, j: (i, j, 0))),
        compiler_params=_cparams("parallel", "parallel"),
        name="out_proj",
    )(x, yl, ya, gl, ga, w, gf, wr, br)


def _route_kernel(lg_ref, ids_ref, gw_ref, cnt_ref, carry):
    tr = lg_ref.shape[0]

    @pl.when(pl.program_id(0) == 0)
    def _():
        carry[...] = jnp.zeros_like(carry)

    lg = lg_ref[...]
    lane = lax.broadcasted_iota(jnp.int32, (tr, ROUTER_LANES), 1)
    big = jnp.int32(ROUTER_LANES)

    def argmax(vals):
        top = jnp.max(vals, axis=-1, keepdims=True)
        return top, jnp.min(jnp.where(vals == top, lane, big), axis=-1, keepdims=True)

    g_logit = jnp.where(lane < N_GROUPS, lg, -jnp.inf)
    g_top, g_idx = argmax(g_logit)
    p_top = 1.0 / jnp.sum(jnp.exp(g_logit - g_top), axis=-1, keepdims=True)
    e_lo = EXPERT_LANE0 + g_idx * EXPERTS_PER_GROUP
    e_logit = jnp.where((lane >= e_lo) & (lane < e_lo + EXPERTS_PER_GROUP), lg, -jnp.inf)
    v1, i1 = argmax(e_logit)
    v2, i2 = argmax(jnp.where(lane == i1, -jnp.inf, e_logit))
    e21 = jnp.exp(v2 - v1)
    w1 = 1.0 / (1.0 + e21) * p_top
    w2 = e21 / (1.0 + e21) * p_top

    hot = ((lane == i1) | (lane == i2))
    rr = lax.broadcasted_iota(jnp.int32, (tr, tr), 0)
    cc = lax.broadcasted_iota(jnp.int32, (tr, tr), 1)
    before = (rr > cc).astype(BF16)
    prefix = jnp.dot(before, hot.astype(BF16), preferred_element_type=F32) + carry[...]
    rank1 = jnp.sum(jnp.where(lane == i1, prefix, 0.0), axis=-1, keepdims=True).astype(jnp.int32)
    rank2 = jnp.sum(jnp.where(lane == i2, prefix, 0.0), axis=-1, keepdims=True).astype(jnp.int32)
    carry[...] = carry[...] + jnp.sum(hot.astype(F32), axis=0, keepdims=True)
    cnt_ref[...] = carry[...]

    ids_ref[...] = jnp.where(lane == 0, i1 - EXPERT_LANE0,
                             jnp.where(lane == 1, i2 - EXPERT_LANE0,
                                       jnp.where(lane == 2, rank1, jnp.where(lane == 3, rank2, 0))))
    gw_ref[...] = jnp.where(lane == 0, w1, jnp.where(lane == 1, w2, 0.0))


def _route(logits):
    t = logits.shape[0]
    tr = min(ROUTE_TILE, t)
    tile = pl.BlockSpec((tr, ROUTER_LANES), lambda i: (i, 0))
    return pl.pallas_call(
        _route_kernel,
        out_shape=(jax.ShapeDtypeStruct((t, ROUTER_LANES), jnp.int32),
                   jax.ShapeDtypeStruct((t, ROUTER_LANES), F32),
                   jax.ShapeDtypeStruct((1, ROUTER_LANES), F32)),
        grid=(t // tr,),
        in_specs=[tile],
        out_specs=(tile, tile, pl.BlockSpec((1, ROUTER_LANES), lambda i: (0, 0))),
        scratch_shapes=[pltpu.VMEM((1, ROUTER_LANES), F32)],
        compiler_params=_cparams("arbitrary"),
        name="route",
    )(logits)


def _row_copy(src_hbm, src_row, dst_ref, dst_row, sem):
    return pltpu.make_async_copy(src_hbm.at[pl.ds(src_row, 1)], dst_ref.at[pl.ds(dst_row, 1)], sem)


def _dispatch_kernel(dest_ref, h2_hbm, zero_hbm, xbuf_hbm, sem):
    del zero_hbm
    n_tok = dest_ref.shape[0] // TOP_K
    t0 = pl.program_id(0) * n_tok

    def start(i, carry):
        for k in range(TOP_K):
            _row_copy(h2_hbm, t0 + i, xbuf_hbm, dest_ref[TOP_K * i + k], sem).start()
        return carry

    def wait(i, carry):
        for k in range(TOP_K):
            _row_copy(h2_hbm, 0, xbuf_hbm, 0, sem).wait()
        return carry

    lax.fori_loop(0, n_tok, start, 0)
    lax.fori_loop(0, n_tok, wait, 0)


def _dispatch(dest, h2, n_rows):
    t, d = h2.shape
    td = min(DMA_TILE, t)
    return pl.pallas_call(
        _dispatch_kernel,
        out_shape=jax.ShapeDtypeStruct((n_rows, d), F32),
        grid=(t // td,),
        in_specs=[pl.BlockSpec((TOP_K * td,), lambda i: (i,), memory_space=pltpu.SMEM),
                  pl.BlockSpec(memory_space=pl.ANY), pl.BlockSpec(memory_space=pl.ANY)],
        out_specs=pl.BlockSpec(memory_space=pl.ANY),
        scratch_shapes=[pltpu.SemaphoreType.DMA],
        input_output_aliases={2: 0},
        compiler_params=_cparams("arbitrary"),
        name="dispatch",
    )(dest, h2, jnp.zeros((n_rows, d), F32))


def _expert_kernel(be_ref, x_ref, wg_ref, wu_ref, wd_ref, y_ref, wg_s, wu_s, wd_s):
    i = pl.program_id(0)
    new_expert = (i == 0) | (be_ref[i] != be_ref[jnp.maximum(i - 1, 0)])

    @pl.when(new_expert)
    def _():
        wg_s[...] = wg_ref[0].astype(BF16)
        wu_s[...] = wu_ref[0].astype(BF16)
        wd_s[...] = wd_ref[0].astype(BF16)

    x = x_ref[...].astype(BF16)
    g = jnp.dot(x, wg_s[...], preferred_element_type=F32)
    u = jnp.dot(x, wu_s[...], preferred_element_type=F32)
    hid = (jax.nn.silu(g) * u).astype(BF16)
    y_ref[...] = jnp.dot(hid, wd_s[...], preferred_element_type=F32)


def _experts(blk_expert, xbuf, wg, wu, wd):
    n_rows, d = xbuf.shape
    de = wg.shape[2]
    n_blk = n_rows // MOE_BLOCK
    rows = pl.BlockSpec((MOE_BLOCK, d), lambda i, be: (i, 0))
    return pl.pallas_call(
        _expert_kernel,
        out_shape=jax.ShapeDtypeStruct((n_rows, d), F32),
        grid_spec=pltpu.PrefetchScalarGridSpec(
            num_scalar_prefetch=1,
            grid=(n_blk,),
            in_specs=[rows,
                      pl.BlockSpec((1, d, de), lambda i, be: (be[i], 0, 0)),
                      pl.BlockSpec((1, d, de), lambda i, be: (be[i], 0, 0)),
                      pl.BlockSpec((1, de, d), lambda i, be: (be[i], 0, 0))],
            out_specs=rows,
            scratch_shapes=[pltpu.VMEM((d, de), BF16), pltpu.VMEM((d, de), BF16), pltpu.VMEM((de, d), BF16)]),
        compiler_params=_cparams("arbitrary"),
        name="experts",
    )(blk_expert, xbuf, wg, wu, wd)


def _combine_kernel(dest_ref, gw_ref, x_ref, ybuf_hbm, o_ref, rows, sem):
    n_tok = x_ref.shape[0]

    def start(i, carry):
        for k in range(TOP_K):
            _row_copy(ybuf_hbm, dest_ref[TOP_K * i + k], rows.at[k], i, sem).start()
        return carry

    def wait(i, carry):
        for k in range(TOP_K):
            _row_copy(ybuf_hbm, 0, rows.at[k], 0, sem).wait()
        return carry

    lax.fori_loop(0, n_tok, start, 0)
    lax.fori_loop(0, n_tok, wait, 0)
    gw = gw_ref[...]
    o_ref[...] = x_ref[...] + (gw[:, 0:1] * rows[0] + gw[:, 1:2] * rows[1])


def _combine(dest, gw, x, ybuf):
    t, d = x.shape
    tc = min(DMA_TILE, t)
    tok = pl.BlockSpec((tc, d), lambda i: (i, 0))
    return pl.pallas_call(
        _combine_kernel,
        out_shape=jax.ShapeDtypeStruct((t, d), F32),
        grid=(t // tc,),
        in_specs=[pl.BlockSpec((TOP_K * tc,), lambda i: (i,), memory_space=pltpu.SMEM),
                  pl.BlockSpec((tc, ROUTER_LANES), lambda i: (i, 0)), tok,
                  pl.BlockSpec(memory_space=pl.ANY)],
        out_specs=tok,
        scratch_shapes=[pltpu.VMEM((TOP_K, tc, d), F32), pltpu.SemaphoreType.DMA],
        compiler_params=_cparams("arbitrary"),
        name="combine",
    )(dest, gw, x, ybuf)


def _block_diag(w):
    nb, bd, _ = w.shape
    eye = jnp.eye(nb, dtype=w.dtype)
    return (eye[:, None, :, None] * w[:, :, None, :]).reshape(nb * bd, nb * bd)


def _moe(x, h2, logits, wg, wu, wd):
    t, d = x.shape
    ids, gw, cnt = _route(logits)
    counts = cnt[0, EXPERT_LANE0:EXPERT_LANE0 + N_EXPERTS].astype(jnp.int32)
    padded = (counts + MOE_BLOCK - 1) // MOE_BLOCK * MOE_BLOCK
    pend = jnp.cumsum(padded)
    pstart = pend - padded
    dest = (pstart[ids[:, 0:TOP_K]] + ids[:, TOP_K:2 * TOP_K]).reshape(t * TOP_K)
    n_rows = t * TOP_K + N_EXPERTS * MOE_BLOCK
    n_blk = n_rows // MOE_BLOCK
    blk_expert = jnp.minimum(
        jnp.searchsorted(pend, jnp.arange(n_blk, dtype=jnp.int32) * MOE_BLOCK, side="right"),
        N_EXPERTS - 1).astype(jnp.int32)
    xbuf = _dispatch(dest, h2, n_rows)
    ybuf = _experts(blk_expert, xbuf, wg, wu, wd)
    return _combine(dest, gw, x, ybuf)


def kernel(x, norm_mix, w_in, conv_w, conv_b, lru_w_a, lru_b_a, lru_w_x, lru_b_x, lru_lambda, q_norm, k_norm, norm_out_lru, norm_out_attn, w_out, norm_ffn, router_group_w, router_group_b, router_expert_w, router_expert_b, w_gate, w_up, w_down):
    b, s, d = x.shape
    depth = w_in.shape[0]
    d_lru = conv_w.shape[2]
    d_attn = norm_out_attn.shape[1]
    pad = ROUTER_LANES - N_GROUPS - N_EXPERTS
    row = lambda v: v.reshape(1, -1)
    for l in range(depth):
        xl, gate, q, k, v = _in_proj(x, row(norm_mix[l]), w_in[l].astype(BF16), d_lru, d_attn)
        y_lru = _lru(xl, gate, conv_w[l], row(conv_b[l]),
                     _block_diag(lru_w_a[l]).astype(BF16), row(lru_b_a[l]),
                     _block_diag(lru_w_x[l]).astype(BF16), row(lru_b_x[l]), row(lru_lambda[l]))
        y_attn = _attention(q, k, v, row(jnp.tile(q_norm[l], HEADS_PER_SLAB)),
                            row(jnp.tile(k_norm[l], HEADS_PER_SLAB)))
        wr = jnp.pad(jnp.concatenate([router_group_w[l], router_expert_w[l]], axis=1), ((0, 0), (0, pad)))
        br = jnp.pad(jnp.concatenate([router_group_b[l], router_expert_b[l]]), (0, pad))
        xn, h2, logits = _out_proj(x, y_lru, y_attn, row(norm_out_lru[l]), row(norm_out_attn[l]),
                                   w_out[l].astype(BF16), row(norm_ffn[l]), wr, row(br))
        t = b * s
        x = _moe(xn.reshape(t, d), h2.reshape(t, d), logits.reshape(t, ROUTER_LANES),
                 w_gate[l], w_up[l], w_down[l]).reshape(b, s, d)
    return x
```

```python
import functools

import jax
import jax.numpy as jnp
from jax import lax
from jax.experimental import pallas as pl
from jax.experimental.pallas import tpu as pltpu

F32 = jnp.float32
BF16 = jnp.bfloat16

EPS = 1e-6
LANES = 128
HEAD_DIM = 64
HEADS_PER_SLAB = LANES // HEAD_DIM
ATTN_BLOCK = 128
DILATIONS = (1, 4, 16)
CONV_WIDTH = 4
LRU_C = 8.0
N_GROUPS = 4
EXPERTS_PER_GROUP = 8
N_EXPERTS = N_GROUPS * EXPERTS_PER_GROUP
TOP_K = 2
ROUTER_LANES = LANES
EXPERT_LANE0 = N_GROUPS
VMEM_LIMIT = 48 * 1024 * 1024

TM_PROJ = 512
LRU_CHUNK = 256
ATTN_UNROLL = 4
NORM_CHUNK = 512
ROUTE_TILE = 512
MOE_BLOCK = 256
DMA_TILE = 256


def _rms(x, g):
    return x * lax.rsqrt(jnp.mean(x * x, axis=-1, keepdims=True) + EPS) * g


def _log1p(z):
    w = 1.0 + z
    return jnp.where(w == 1.0, z, z * jnp.log(w) / jnp.where(w == 1.0, 1.0, w - 1.0))


def _softplus(x):
    return jnp.maximum(x, 0.0) + _log1p(jnp.exp(-jnp.abs(x)))


def _one_minus_exp(x, exp_x):
    u = exp_x
    near = (1.0 - u) * x / jnp.log(jnp.where(u == 1.0, 0.5, u))
    near = jnp.where(u == 1.0, -x, near)
    return jnp.where(x > -0.5, near, 1.0 - u)


def _cparams(*sem):
    return pltpu.CompilerParams(dimension_semantics=sem, vmem_limit_bytes=VMEM_LIMIT)


def _in_proj_kernel(x_ref, g_ref, w_ref, xl_ref, gate_ref, q_ref, k_ref, v_ref, *, d_lru, d_attn):
    h = _rms(x_ref[0], g_ref[...]).astype(BF16)

    def cols(lo, n):
        return jnp.dot(h, w_ref[:, lo:lo + n], preferred_element_type=F32)

    xl_ref[0] = cols(0, d_lru)
    gate_ref[0] = cols(d_lru, d_lru)
    base = 2 * d_lru
    for ref in (q_ref, k_ref, v_ref):
        for j in range(d_attn // LANES):
            ref[0, j] = cols(base + j * LANES, LANES)
        base += d_attn


def _in_proj(x, g, w, d_lru, d_attn):
    b, s, d = x.shape
    tm = min(TM_PROJ, s)
    n_slab = d_attn // LANES
    slab = jax.ShapeDtypeStruct((b, n_slab, s, LANES), F32)
    row = jax.ShapeDtypeStruct((b, s, d_lru), F32)
    slab_spec = pl.BlockSpec((1, n_slab, tm, LANES), lambda i, j: (i, 0, j, 0))
    row_spec = pl.BlockSpec((1, tm, d_lru), lambda i, j: (i, j, 0))
    return pl.pallas_call(
        functools.partial(_in_proj_kernel, d_lru=d_lru, d_attn=d_attn),
        out_shape=(row, row, slab, slab, slab),
        grid=(b, s // tm),
        in_specs=[pl.BlockSpec((1, tm, d), lambda i, j: (i, j, 0)),
                  pl.BlockSpec((1, d), lambda i, j: (0, 0)),
                  pl.BlockSpec(w.shape, lambda i, j: (0, 0))],
        out_specs=(row_spec, row_spec, slab_spec, slab_spec, slab_spec),
        compiler_params=_cparams("parallel", "parallel"),
        name="in_proj",
    )(x, g, w)


def _lru_kernel(xl_ref, gate_ref, cw_ref, cb_ref, wa_ref, ba_ref, wx_ref, bx_ref, lam_ref,
                y_ref, xcat, h_scr):
    tc = xl_ref.shape[1]
    halo = 8

    @pl.when(pl.program_id(1) == 0)
    def _():
        xcat[0:halo, :] = jnp.zeros((halo, xcat.shape[1]), F32)
        h_scr[...] = jnp.zeros_like(h_scr)

    x = xl_ref[0]
    xcat[halo:halo + tc, :] = x
    xc = cb_ref[...]
    for j in range(CONV_WIDTH):
        xc = xc + cw_ref[j:j + 1, :] * xcat[pl.ds(halo - (CONV_WIDTH - 1) + j, tc), :]
    xcat[0:halo, :] = x[tc - halo:tc, :]

    xb = xc.astype(BF16)
    r = jax.nn.sigmoid(jnp.dot(xb, wa_ref[...], preferred_element_type=F32) + ba_ref[...])
    i = jax.nn.sigmoid(jnp.dot(xb, wx_ref[...], preferred_element_type=F32) + bx_ref[...])
    log_a = -LRU_C * r * _softplus(-lam_ref[...])
    a = jnp.exp(log_a)
    u = jnp.sqrt(_one_minus_exp(2.0 * log_a, a * a)) * (i * xc)

    row = lax.broadcasted_iota(jnp.int32, (tc, 1), 0)
    shift = 1
    while shift < tc:
        keep = row >= shift
        a_prev = pltpu.roll(a, shift, axis=0)
        u_prev = pltpu.roll(u, shift, axis=0)
        u = jnp.where(keep, a * u_prev + u, u)
        a = jnp.where(keep, a * a_prev, a)
        shift *= 2
    h = a * h_scr[...] + u
    h_scr[...] = h[tc - 1:tc, :]
    y_ref[0] = h * jax.nn.gelu(gate_ref[0])


def _lru(xl, gate, cw, cb, wa, ba, wx, bx, lam):
    b, s, c = xl.shape
    tc = min(LRU_CHUNK, s)
    row_spec = pl.BlockSpec((1, tc, c), lambda i, j: (i, j, 0))
    vec = pl.BlockSpec((1, c), lambda i, j: (0, 0))
    mat = pl.BlockSpec((c, c), lambda i, j: (0, 0))
    return pl.pallas_call(
        _lru_kernel,
        out_shape=jax.ShapeDtypeStruct((b, s, c), F32),
        grid=(b, s // tc),
        in_specs=[row_spec, row_spec, pl.BlockSpec((CONV_WIDTH, c), lambda i, j: (0, 0)), vec,
                  mat, vec, mat, vec, vec],
        out_specs=row_spec,
        scratch_shapes=[pltpu.VMEM((tc + 8, c), F32), pltpu.VMEM((1, c), F32)],
        compiler_params=_cparams("parallel", "arbitrary"),
        name="rg_lru",
    )(xl, gate, cw, cb, wa, ba, wx, bx, lam)


def _attn_kernel(q_ref, k_ref, v_ref, qg_ref, kg_ref, o_ref, qn, kn, m_s, l_s, acc_s):
    s_len = q_ref.shape[2]
    blk = ATTN_BLOCK
    lane = lax.broadcasted_iota(jnp.int32, (1, LANES), 1)
    lo = lane < HEAD_DIM
    n_chunk = s_len // NORM_CHUNK

    def norm_body(c, carry):
        rows = pl.ds(pl.multiple_of(c * NORM_CHUNK, NORM_CHUNK), NORM_CHUNK)
        for src, g_ref, dst, scale in ((q_ref, qg_ref, qn, HEAD_DIM ** -0.5), (k_ref, kg_ref, kn, 1.0)):
            x = src[0, 0, rows, :]
            x2 = x * x
            s0 = jnp.sum(jnp.where(lo, x2, 0.0), axis=-1, keepdims=True)
            s1 = jnp.sum(jnp.where(lo, 0.0, x2), axis=-1, keepdims=True)
            ms = jnp.where(lo, s0, s1) * (1.0 / HEAD_DIM)
            dst[rows, :] = x * lax.rsqrt(ms + EPS) * g_ref[...] * scale
        return carry

    lax.fori_loop(0, n_chunk, norm_body, 0)

    rows2 = lax.broadcasted_iota(jnp.int32, (HEADS_PER_SLAB * blk, blk), 0)
    qi = rows2 & (blk - 1)
    kj = lax.broadcasted_iota(jnp.int32, (HEADS_PER_SLAB * blk, blk), 1)
    band_prev = kj >= qi
    band_cur = kj <= qi
    nt = (((1,), (1,)), ((), ()))

    def branch(d, first_branch):
        log_d = d.bit_length() - 1
        span = blk * d

        def sl(start):
            return pl.ds(start, blk) if d == 1 else pl.ds(start, blk, stride=d)

        def body(it, carry):
            n = it >> log_d
            r = it & (d - 1)
            cur = sl(n * span + r)
            prev = sl(jnp.maximum(n - 1, 0) * span + r)
            qt = qn[cur, :]
            qs = jnp.concatenate([jnp.where(lo, qt, 0.0), jnp.where(lo, 0.0, qt)], axis=0).astype(BF16)
            s_c = lax.dot_general(qs, kn[cur, :].astype(BF16), nt, preferred_element_type=F32)
            s_p = lax.dot_general(qs, kn[prev, :].astype(BF16), nt, preferred_element_type=F32)
            s_c = jnp.where(band_cur, s_c, -jnp.inf)
            s_p = jnp.where(band_prev & (n > 0), s_p, -jnp.inf)
            m = jnp.max(jnp.maximum(s_c, s_p), axis=-1, keepdims=True)
            p_c = jnp.exp(s_c - m)
            p_p = jnp.exp(s_p - m)
            l = jnp.sum(p_c + p_p, axis=-1, keepdims=True)
            acc = (jnp.dot(p_c.astype(BF16), v_ref[0, 0, cur, :].astype(BF16), preferred_element_type=F32)
                   + jnp.dot(p_p.astype(BF16), v_ref[0, 0, prev, :].astype(BF16), preferred_element_type=F32))
            acc_t = jnp.where(lo, acc[:blk], acc[blk:])
            m_t = jnp.where(lo, jnp.broadcast_to(m[:blk], (blk, LANES)), jnp.broadcast_to(m[blk:], (blk, LANES)))
            l_t = jnp.where(lo, jnp.broadcast_to(l[:blk], (blk, LANES)), jnp.broadcast_to(l[blk:], (blk, LANES)))
            if first_branch:
                m_s[cur, :] = m_t
                l_s[cur, :] = l_t
                acc_s[cur, :] = acc_t
            else:
                m_o = m_s[cur, :]
                m_n = jnp.maximum(m_o, m_t)
                c_o = jnp.exp(m_o - m_n)
                c_t = jnp.exp(m_t - m_n)
                m_s[cur, :] = m_n
                l_s[cur, :] = l_s[cur, :] * c_o + l_t * c_t
                acc_s[cur, :] = acc_s[cur, :] * c_o + acc_t * c_t
            return carry

        lax.fori_loop(0, s_len // blk, body, 0, unroll=ATTN_UNROLL)

    for idx, d in enumerate(DILATIONS):
        branch(d, idx == 0)

    def out_body(c, carry):
        rows = pl.ds(pl.multiple_of(c * NORM_CHUNK, NORM_CHUNK), NORM_CHUNK)
        o_ref[0, 0, rows, :] = acc_s[rows, :] / l_s[rows, :]
        return carry

    lax.fori_loop(0, n_chunk, out_body, 0)


def _attention(q, k, v, qg, kg):
    b, n_slab, s, _ = q.shape
    assert s % (ATTN_BLOCK * max(DILATIONS)) == 0 and s % NORM_CHUNK == 0
    slab_spec = pl.BlockSpec((1, 1, s, LANES), lambda i, j: (i, j, 0, 0))
    vec = pl.BlockSpec((1, LANES), lambda i, j: (0, 0))
    return pl.pallas_call(
        _attn_kernel,
        out_shape=jax.ShapeDtypeStruct(q.shape, F32),
        grid=(b, n_slab),
        in_specs=[slab_spec, slab_spec, slab_spec, vec, vec],
        out_specs=slab_spec,
        scratch_shapes=[pltpu.VMEM((s, LANES), F32) for _ in range(5)],
        compiler_params=_cparams("parallel", "parallel"),
        name="dilated_attn",
    )(q, k, v, qg, kg)


def _out_proj_kernel(x_ref, yl_ref, ya_ref, gl_ref, ga_ref, w_ref, gf_ref, wr_ref, br_ref,
                     xn_ref, h2_ref, lg_ref):
    d_lru = yl_ref.shape[2]
    nl = _rms(yl_ref[0], gl_ref[...]).astype(BF16)
    ya = jnp.concatenate([ya_ref[0, j] for j in range(ya_ref.shape[1])], axis=-1)
    na = _rms(ya, ga_ref[...]).astype(BF16)
    xn = x_ref[0] + (jnp.dot(nl, w_ref[0:d_lru, :], preferred_element_type=F32)
                     + jnp.dot(na, w_ref[d_lru:, :], preferred_element_type=F32))
    xn_ref[0] = xn
    h2 = _rms(xn, gf_ref[...])
    h2_ref[0] = h2
    lg_ref[0] = jnp.dot(h2, wr_ref[...], preferred_element_type=F32,
                        precision=lax.Precision.HIGHEST) + br_ref[...]


def _out_proj(x, yl, ya, gl, ga, w, gf, wr, br):
    b, s, d = x.shape
    d_lru = yl.shape[2]
    n_slab = ya.shape[1]
    tm = min(TM_PROJ, s)
    tok = pl.BlockSpec((1, tm, d), lambda i, j: (i, j, 0))
    const = lambda shape: pl.BlockSpec(shape, lambda i, j: (0,) * len(shape))
    return pl.pallas_call(
        _out_proj_kernel,
        out_shape=(jax.ShapeDtypeStruct((b, s, d), F32), jax.ShapeDtypeStruct((b, s, d), F32),
                   jax.ShapeDtypeStruct((b, s, ROUTER_LANES), F32)),
        grid=(b, s // tm),
        in_specs=[tok, pl.BlockSpec((1, tm, d_lru), lambda i, j: (i, j, 0)),
                  pl.BlockSpec((1, n_slab, tm, LANES), lambda i, j: (i, 0, j, 0)),
                  const((1, d_lru)), const((1, n_slab * LANES)), const(w.shape), const((1, d)),
                  const(wr.shape), const((1, ROUTER_LANES))],
        out_specs=(tok, tok, pl.BlockSpec((1, tm, ROUTER_LANES), lambda i, j: (i, j, 0))),
        compiler_params=_cparams("parallel", "parallel"),
        name="out_proj",
    )(x, yl, ya, gl, ga, w, gf, wr, br)


def _route_kernel(lg_ref, ids_ref, gw_ref, cnt_ref, carry):
    tr = lg_ref.shape[0]

    @pl.when(pl.program_id(0) == 0)
    def _():
        carry[...] = jnp.zeros_like(carry)

    lg = lg_ref[...]
    lane = lax.broadcasted_iota(jnp.int32, (tr, ROUTER_LANES), 1)
    big = jnp.int32(ROUTER_LANES)

    def argmax(vals):
        top = jnp.max(vals, axis=-1, keepdims=True)
        return top, jnp.min(jnp.where(vals == top, lane, big), axis=-1, keepdims=True)

    g_logit = jnp.where(lane < N_GROUPS, lg, -jnp.inf)
    g_top, g_idx = argmax(g_logit)
    p_top = 1.0 / jnp.sum(jnp.exp(g_logit - g_top), axis=-1, keepdims=True)
    e_lo = EXPERT_LANE0 + g_idx * EXPERTS_PER_GROUP
    e_logit = jnp.where((lane >= e_lo) & (lane < e_lo + EXPERTS_PER_GROUP), lg, -jnp.inf)
    v1, i1 = argmax(e_logit)
    v2, i2 = argmax(jnp.where(lane == i1, -jnp.inf, e_logit))
    e21 = jnp.exp(v2 - v1)
    w1 = 1.0 / (1.0 + e21) * p_top
    w2 = e21 / (1.0 + e21) * p_top

    hot = ((lane == i1) | (lane == i2))
    rr = lax.broadcasted_iota(jnp.int32, (tr, tr), 0)
    cc = lax.broadcasted_iota(jnp.int32, (tr, tr), 1)
    before = (rr > cc).astype(BF16)
    prefix = jnp.dot(before, hot.astype(BF16), preferred_element_type=F32) + carry[...]
    rank1 = jnp.sum(jnp.where(lane == i1, prefix, 0.0), axis=-1, keepdims=True).astype(jnp.int32)
    rank2 = jnp.sum(jnp.where(lane == i2, prefix, 0.0), axis=-1, keepdims=True).astype(jnp.int32)
    carry[...] = carry[...] + jnp.sum(hot.astype(F32), axis=0, keepdims=True)
    cnt_ref[...] = carry[...]

    ids_ref[...] = jnp.where(lane == 0, i1 - EXPERT_LANE0,
                             jnp.where(lane == 1, i2 - EXPERT_LANE0,
                                       jnp.where(lane == 2, rank1, jnp.where(lane == 3, rank2, 0))))
    gw_ref[...] = jnp.where(lane == 0, w1, jnp.where(lane == 1, w2, 0.0))


def _route(logits):
    t = logits.shape[0]
    tr = min(ROUTE_TILE, t)
    tile = pl.BlockSpec((tr, ROUTER_LANES), lambda i: (i, 0))
    return pl.pallas_call(
        _route_kernel,
        out_shape=(jax.ShapeDtypeStruct((t, ROUTER_LANES), jnp.int32),
                   jax.ShapeDtypeStruct((t, ROUTER_LANES), F32),
                   jax.ShapeDtypeStruct((1, ROUTER_LANES), F32)),
        grid=(t // tr,),
        in_specs=[tile],
        out_specs=(tile, tile, pl.BlockSpec((1, ROUTER_LANES), lambda i: (0, 0))),
        scratch_shapes=[pltpu.VMEM((1, ROUTER_LANES), F32)],
        compiler_params=_cparams("arbitrary"),
        name="route",
    )(logits)


def _row_copy(src_ref, src_row, dst_ref, dst_row, sem):
    return pltpu.make_async_copy(src_ref.at[pl.ds(src_row, 1)], dst_ref.at[pl.ds(dst_row, 1)], sem)


def _dispatch_kernel(dest_ref, h2_ref, zero_hbm, xbuf_hbm, sem):
    del zero_hbm
    n_tok = h2_ref.shape[0]

    def start(i, carry):
        for k in range(TOP_K):
            _row_copy(h2_ref, i, xbuf_hbm, dest_ref[TOP_K * i + k], sem).start()
        return carry

    def wait(i, carry):
        for k in range(TOP_K):
            _row_copy(h2_ref, 0, xbuf_hbm, 0, sem).wait()
        return carry

    lax.fori_loop(0, n_tok, start, 0)
    lax.fori_loop(0, n_tok, wait, 0)


def _dispatch(dest, h2, n_rows):
    t, d = h2.shape
    td = min(DMA_TILE, t)
    return pl.pallas_call(
        _dispatch_kernel,
        out_shape=jax.ShapeDtypeStruct((n_rows, d), F32),
        grid=(t // td,),
        in_specs=[pl.BlockSpec((TOP_K * td,), lambda i: (i,), memory_space=pltpu.SMEM),
                  pl.BlockSpec((td, d), lambda i: (i, 0)), pl.BlockSpec(memory_space=pl.ANY)],
        out_specs=pl.BlockSpec(memory_space=pl.ANY),
        scratch_shapes=[pltpu.SemaphoreType.DMA],
        input_output_aliases={2: 0},
        compiler_params=_cparams("arbitrary"),
        name="dispatch",
    )(dest, h2, jnp.zeros((n_rows, d), F32))


def _expert_kernel(be_ref, x_ref, wg_ref, wu_ref, wd_ref, y_ref, wg_s, wu_s, wd_s):
    i = pl.program_id(0)
    new_expert = (i == 0) | (be_ref[i] != be_ref[jnp.maximum(i - 1, 0)])

    @pl.when(new_expert)
    def _():
        wg_s[...] = wg_ref[0, 0].astype(BF16)
        wu_s[...] = wu_ref[0, 0].astype(BF16)
        wd_s[...] = wd_ref[0, 0].astype(BF16)

    x = x_ref[...].astype(BF16)
    g = jnp.dot(x, wg_s[...], preferred_element_type=F32)
    u = jnp.dot(x, wu_s[...], preferred_element_type=F32)
    hid = (jax.nn.silu(g) * u).astype(BF16)
    y_ref[...] = jnp.dot(hid, wd_s[...], preferred_element_type=F32)


def _experts(blk_expert, xbuf, wg, wu, wd, layer):
    n_rows, d = xbuf.shape
    de = wg.shape[3]
    n_blk = n_rows // MOE_BLOCK
    rows = pl.BlockSpec((MOE_BLOCK, d), lambda i, be: (i, 0))
    return pl.pallas_call(
        _expert_kernel,
        out_shape=jax.ShapeDtypeStruct((n_rows, d), F32),
        grid_spec=pltpu.PrefetchScalarGridSpec(
            num_scalar_prefetch=1,
            grid=(n_blk,),
            in_specs=[rows,
                      pl.BlockSpec((1, 1, d, de), lambda i, be: (layer, be[i], 0, 0)),
                      pl.BlockSpec((1, 1, d, de), lambda i, be: (layer, be[i], 0, 0)),
                      pl.BlockSpec((1, 1, de, d), lambda i, be: (layer, be[i], 0, 0))],
            out_specs=rows,
            scratch_shapes=[pltpu.VMEM((d, de), BF16), pltpu.VMEM((d, de), BF16), pltpu.VMEM((de, d), BF16)]),
        compiler_params=_cparams("arbitrary"),
        name="experts",
    )(blk_expert, xbuf, wg, wu, wd)


def _combine_kernel(dest_ref, gw_ref, x_ref, ybuf_hbm, o_ref, rows, sem):
    n_tok = x_ref.shape[0]

    def start(i, carry):
        for k in range(TOP_K):
            _row_copy(ybuf_hbm, dest_ref[TOP_K * i + k], rows.at[k], i, sem).start()
        return carry

    def wait(i, carry):
        for k in range(TOP_K):
            _row_copy(ybuf_hbm, 0, rows.at[k], 0, sem).wait()
        return carry

    lax.fori_loop(0, n_tok, start, 0)
    lax.fori_loop(0, n_tok, wait, 0)
    gw = gw_ref[...]
    o_ref[...] = x_ref[...] + (gw[:, 0:1] * rows[0] + gw[:, 1:2] * rows[1])


def _combine(dest, gw, x, ybuf):
    t, d = x.shape
    tc = min(DMA_TILE, t)
    tok = pl.BlockSpec((tc, d), lambda i: (i, 0))
    return pl.pallas_call(
        _combine_kernel,
        out_shape=jax.ShapeDtypeStruct((t, d), F32),
        grid=(t // tc,),
        in_specs=[pl.BlockSpec((TOP_K * tc,), lambda i: (i,), memory_space=pltpu.SMEM),
                  pl.BlockSpec((tc, ROUTER_LANES), lambda i: (i, 0)), tok,
                  pl.BlockSpec(memory_space=pl.ANY)],
        out_specs=tok,
        scratch_shapes=[pltpu.VMEM((TOP_K, tc, d), F32), pltpu.SemaphoreType.DMA],
        compiler_params=_cparams("arbitrary"),
        name="combine",
    )(dest, gw, x, ybuf)


def _block_diag(w):
    nb, bd, _ = w.shape
    eye = jnp.eye(nb, dtype=w.dtype)
    return (eye[:, None, :, None] * w[:, :, None, :]).reshape(nb * bd, nb * bd)


def _moe(x, h2, logits, wg, wu, wd, layer):
    t, d = x.shape
    ids, gw, cnt = _route(logits)
    counts = cnt[0, EXPERT_LANE0:EXPERT_LANE0 + N_EXPERTS].astype(jnp.int32)
    padded = (counts + MOE_BLOCK - 1) // MOE_BLOCK * MOE_BLOCK
    pend = jnp.cumsum(padded)
    pstart = pend - padded
    dest = (pstart[ids[:, 0:TOP_K]] + ids[:, TOP_K:2 * TOP_K]).reshape(t * TOP_K)
    n_rows = t * TOP_K + N_EXPERTS * MOE_BLOCK
    n_blk = n_rows // MOE_BLOCK
    blk_start = jnp.arange(n_blk, dtype=jnp.int32) * MOE_BLOCK
    blk_expert = jnp.minimum(jnp.sum(pend[None, :] <= blk_start[:, None], axis=1), N_EXPERTS - 1).astype(jnp.int32)
    xbuf = _dispatch(dest, h2, n_rows)
    ybuf = _experts(blk_expert, xbuf, wg, wu, wd, layer)
    return _combine(dest, gw, x, ybuf)


def kernel(x, norm_mix, w_in, conv_w, conv_b, lru_w_a, lru_b_a, lru_w_x, lru_b_x, lru_lambda, q_norm, k_norm, norm_out_lru, norm_out_attn, w_out, norm_ffn, router_group_w, router_group_b, router_expert_w, router_expert_b, w_gate, w_up, w_down):
    b, s, d = x.shape
    depth = w_in.shape[0]
    d_lru = conv_w.shape[2]
    d_attn = norm_out_attn.shape[1]
    pad = ROUTER_LANES - N_GROUPS - N_EXPERTS
    row = lambda v: v.reshape(1, -1)
    for l in range(depth):
        xl, gate, q, k, v = _in_proj(x, row(norm_mix[l]), w_in[l].astype(BF16), d_lru, d_attn)
        y_lru = _lru(xl, gate, conv_w[l], row(conv_b[l]),
                     _block_diag(lru_w_a[l]).astype(BF16), row(lru_b_a[l]),
                     _block_diag(lru_w_x[l]).astype(BF16), row(lru_b_x[l]), row(lru_lambda[l]))
        y_attn = _attention(q, k, v, row(jnp.tile(q_norm[l], HEADS_PER_SLAB)),
                            row(jnp.tile(k_norm[l], HEADS_PER_SLAB)))
        wr = jnp.pad(jnp.concatenate([router_group_w[l], router_expert_w[l]], axis=1), ((0, 0), (0, pad)))
        br = jnp.pad(jnp.concatenate([router_group_b[l], router_expert_b[l]]), (0, pad))
        xn, h2, logits = _out_proj(x, y_lru, y_attn, row(norm_out_lru[l]), row(norm_out_attn[l]),
                                   w_out[l].astype(BF16), row(norm_ffn[l]), wr, row(br))
        t = b * s
        x = _moe(xn.reshape(t, d), h2.reshape(t, d), logits.reshape(t, ROUTER_LANES),
                 w_gate, w_up, w_down, l).reshape(b, s, d)
    return x
```

```python
import functools

import jax
import jax.numpy as jnp
from jax import lax
from jax.experimental import pallas as pl
from jax.experimental.pallas import tpu as pltpu

F32 = jnp.float32
BF16 = jnp.bfloat16

EPS = 1e-6
LANES = 128
HEAD_DIM = 64
HEADS_PER_SLAB = LANES // HEAD_DIM
ATTN_BLOCK = 128
DILATIONS = (1, 4, 16)
CONV_WIDTH = 4
LRU_C = 8.0
N_GROUPS = 4
EXPERTS_PER_GROUP = 8
N_EXPERTS = N_GROUPS * EXPERTS_PER_GROUP
TOP_K = 2
ROUTER_LANES = LANES
EXPERT_LANE0 = N_GROUPS
VMEM_LIMIT = 56 * 1024 * 1024

TM_PROJ = 512
LRU_CHUNK = 256
PAD_GROUP = 16
PAD_PITCH = 24
ATTN_UNROLL = 4
NORM_CHUNK = 512
ROUTE_TILE = 512
MOE_BLOCK = 256
DMA_TILE = 1024
DMA_UNROLL = 8


def _rms(x, g):
    return x * lax.rsqrt(jnp.mean(x * x, axis=-1, keepdims=True) + EPS) * g


def _log1p(z):
    w = 1.0 + z
    return jnp.where(w == 1.0, z, z * jnp.log(w) / jnp.where(w == 1.0, 1.0, w - 1.0))


def _softplus(x):
    return jnp.maximum(x, 0.0) + _log1p(jnp.exp(-jnp.abs(x)))


def _one_minus_exp(x, exp_x):
    u = exp_x
    near = (1.0 - u) * x / jnp.log(jnp.where(u == 1.0, 0.5, u))
    near = jnp.where(u == 1.0, -x, near)
    return jnp.where(x > -0.5, near, 1.0 - u)


def _cparams(*sem):
    return pltpu.CompilerParams(dimension_semantics=sem, vmem_limit_bytes=VMEM_LIMIT)


def _in_proj_kernel(x_ref, g_ref, w_ref, xl_ref, gate_ref, q_ref, k_ref, v_ref, *, d_lru, d_attn):
    h = _rms(x_ref[0], g_ref[...]).astype(BF16)

    def cols(lo, n):
        return jnp.dot(h, w_ref[:, lo:lo + n], preferred_element_type=F32)

    xl_ref[0] = cols(0, d_lru)
    gate_ref[0] = cols(d_lru, d_lru)
    base = 2 * d_lru
    for ref in (q_ref, k_ref, v_ref):
        for j in range(d_attn // LANES):
            ref[0, j] = cols(base + j * LANES, LANES)
        base += d_attn


def _in_proj(x, g, w, d_lru, d_attn):
    b, s, d = x.shape
    tm = min(TM_PROJ, s)
    n_slab = d_attn // LANES
    slab = jax.ShapeDtypeStruct((b, n_slab, s, LANES), F32)
    row = jax.ShapeDtypeStruct((b, s, d_lru), F32)
    slab_spec = pl.BlockSpec((1, n_slab, tm, LANES), lambda i, j: (i, 0, j, 0))
    row_spec = pl.BlockSpec((1, tm, d_lru), lambda i, j: (i, j, 0))
    return pl.pallas_call(
        functools.partial(_in_proj_kernel, d_lru=d_lru, d_attn=d_attn),
        out_shape=(row, row, slab, slab, slab),
        grid=(b, s // tm),
        in_specs=[pl.BlockSpec((1, tm, d), lambda i, j: (i, j, 0)),
                  pl.BlockSpec((1, d), lambda i, j: (0, 0)),
                  pl.BlockSpec(w.shape, lambda i, j: (0, 0))],
        out_specs=(row_spec, row_spec, slab_spec, slab_spec, slab_spec),
        compiler_params=_cparams("parallel", "parallel"),
        name="in_proj",
    )(x, g, w)


def _lru_kernel(xl_ref, gate_ref, cw_ref, cb_ref, wa_ref, ba_ref, wx_ref, bx_ref, lam_ref,
                y_ref, xcat, h_scr):
    tc = xl_ref.shape[1]
    halo = 8

    @pl.when(pl.program_id(1) == 0)
    def _():
        xcat[0:halo, :] = jnp.zeros((halo, xcat.shape[1]), F32)
        h_scr[...] = jnp.zeros_like(h_scr)

    x = xl_ref[0]
    xcat[halo:halo + tc, :] = x
    xc = cb_ref[...]
    for j in range(CONV_WIDTH):
        xc = xc + cw_ref[j:j + 1, :] * xcat[pl.ds(halo - (CONV_WIDTH - 1) + j, tc), :]
    xcat[0:halo, :] = x[tc - halo:tc, :]

    xb = xc.astype(BF16)
    r = jax.nn.sigmoid(jnp.dot(xb, wa_ref[...], preferred_element_type=F32) + ba_ref[...])
    i = jax.nn.sigmoid(jnp.dot(xb, wx_ref[...], preferred_element_type=F32) + bx_ref[...])
    log_a = -LRU_C * r * _softplus(-lam_ref[...])
    a = jnp.exp(log_a)
    u = jnp.sqrt(_one_minus_exp(2.0 * log_a, a * a)) * (i * xc)

    row = lax.broadcasted_iota(jnp.int32, (tc, 1), 0)
    shift = 1
    while shift < tc:
        keep = row >= shift
        a_prev = pltpu.roll(a, shift, axis=0)
        u_prev = pltpu.roll(u, shift, axis=0)
        u = jnp.where(keep, a * u_prev + u, u)
        a = jnp.where(keep, a * a_prev, a)
        shift *= 2
    h = a * h_scr[...] + u
    h_scr[...] = h[tc - 1:tc, :]
    y_ref[0] = h * jax.nn.gelu(gate_ref[0])


def _lru(xl, gate, cw, cb, wa, ba, wx, bx, lam):
    b, s, c = xl.shape
    tc = min(LRU_CHUNK, s)
    row_spec = pl.BlockSpec((1, tc, c), lambda i, j: (i, j, 0))
    vec = pl.BlockSpec((1, c), lambda i, j: (0, 0))
    mat = pl.BlockSpec((c, c), lambda i, j: (0, 0))
    return pl.pallas_call(
        _lru_kernel,
        out_shape=jax.ShapeDtypeStruct((b, s, c), F32),
        grid=(b, s // tc),
        in_specs=[row_spec, row_spec, pl.BlockSpec((CONV_WIDTH, c), lambda i, j: (0, 0)), vec,
                  mat, vec, mat, vec, vec],
        out_specs=row_spec,
        scratch_shapes=[pltpu.VMEM((tc + 8, c), F32), pltpu.VMEM((1, c), F32)],
        compiler_params=_cparams("parallel", "arbitrary"),
        name="rg_lru",
    )(xl, gate, cw, cb, wa, ba, wx, bx, lam)


def _attn_kernel(q_ref, k_ref, v_ref, qg_ref, kg_ref, o_ref,
                 qn, kn, m_s, l_s, acc_s, qp, kp, vp, m_p, l_p, acc_p):
    s_len = q_ref.shape[2]
    blk = ATTN_BLOCK
    lane = lax.broadcasted_iota(jnp.int32, (1, LANES), 1)
    lo = lane < HEAD_DIM
    n_chunk = s_len // NORM_CHUNK
    groups = NORM_CHUNK // PAD_GROUP

    def padded_base(c):
        return pl.multiple_of(c * (groups * PAD_PITCH), 8)

    def to_padded(dst, c, val):
        for g in range(groups):
            dst[pl.ds(padded_base(c) + g * PAD_PITCH, PAD_GROUP), :] = val[g * PAD_GROUP:(g + 1) * PAD_GROUP]

    def from_padded(src, c):
        return jnp.concatenate(
            [src[pl.ds(padded_base(c) + g * PAD_PITCH, PAD_GROUP), :] for g in range(groups)], axis=0)

    def norm_body(c, carry):
        rows = pl.ds(pl.multiple_of(c * NORM_CHUNK, NORM_CHUNK), NORM_CHUNK)
        for src, g_ref, dst, dst_p, scale in ((q_ref, qg_ref, qn, qp, HEAD_DIM ** -0.5),
                                              (k_ref, kg_ref, kn, kp, 1.0)):
            x = src[0, 0, rows, :]
            x2 = x * x
            s0 = jnp.sum(jnp.where(lo, x2, 0.0), axis=-1, keepdims=True)
            s1 = jnp.sum(jnp.where(lo, 0.0, x2), axis=-1, keepdims=True)
            ms = jnp.where(lo, s0, s1) * (1.0 / HEAD_DIM)
            y = x * lax.rsqrt(ms + EPS) * g_ref[...] * scale
            dst[rows, :] = y
            to_padded(dst_p, c, y)
        to_padded(vp, c, v_ref[0, 0, rows, :])
        return carry

    lax.fori_loop(0, n_chunk, norm_body, 0)

    rows2 = lax.broadcasted_iota(jnp.int32, (HEADS_PER_SLAB * blk, blk), 0)
    qi = rows2 & (blk - 1)
    kj = lax.broadcasted_iota(jnp.int32, (HEADS_PER_SLAB * blk, blk), 1)
    band_prev = kj >= qi
    band_cur = kj <= qi
    nt = (((1,), (1,)), ((), ()))

    def branch(d, q_src, k_src, v_src, stats, merge):
        m_r, l_r, acc_r = stats
        log_d = d.bit_length() - 1
        padded = d == PAD_GROUP
        pitch = PAD_PITCH if padded else d
        span = blk * pitch

        def sl(start):
            return pl.ds(start, blk) if pitch == 1 else pl.ds(start, blk, stride=pitch)

        def body(it, carry):
            n = it >> log_d
            r = it & (d - 1)
            cur = sl(n * span + r)
            prev = sl(jnp.maximum(n - 1, 0) * span + r)
            qt = q_src(cur)
            qs = jnp.concatenate([jnp.where(lo, qt, 0.0), jnp.where(lo, 0.0, qt)], axis=0).astype(BF16)
            s_c = lax.dot_general(qs, k_src(cur).astype(BF16), nt, preferred_element_type=F32)
            s_p = lax.dot_general(qs, k_src(prev).astype(BF16), nt, preferred_element_type=F32)
            s_c = jnp.where(band_cur, s_c, -jnp.inf)
            s_p = jnp.where(band_prev & (n > 0), s_p, -jnp.inf)
            m = jnp.max(jnp.maximum(s_c, s_p), axis=-1, keepdims=True)
            p_c = jnp.exp(s_c - m)
            p_p = jnp.exp(s_p - m)
            l = jnp.sum(p_c + p_p, axis=-1, keepdims=True)
            acc = (jnp.dot(p_c.astype(BF16), v_src(cur).astype(BF16), preferred_element_type=F32)
                   + jnp.dot(p_p.astype(BF16), v_src(prev).astype(BF16), preferred_element_type=F32))
            acc_t = jnp.where(lo, acc[:blk], acc[blk:])
            m_t = jnp.where(lo, jnp.broadcast_to(m[:blk], (blk, LANES)), jnp.broadcast_to(m[blk:], (blk, LANES)))
            l_t = jnp.where(lo, jnp.broadcast_to(l[:blk], (blk, LANES)), jnp.broadcast_to(l[blk:], (blk, LANES)))
            if merge:
                m_o = m_r[cur, :]
                m_n = jnp.maximum(m_o, m_t)
                c_o = jnp.exp(m_o - m_n)
                c_t = jnp.exp(m_t - m_n)
                m_r[cur, :] = m_n
                l_r[cur, :] = l_r[cur, :] * c_o + l_t * c_t
                acc_r[cur, :] = acc_r[cur, :] * c_o + acc_t * c_t
            else:
                m_r[cur, :] = m_t
                l_r[cur, :] = l_t
                acc_r[cur, :] = acc_t
            return carry

        lax.fori_loop(0, s_len // blk, body, 0, unroll=ATTN_UNROLL)

    token_order = (lambda s: qn[s, :], lambda s: kn[s, :], lambda s: v_ref[0, 0, s, :], (m_s, l_s, acc_s))
    padded_order = (lambda s: qp[s, :], lambda s: kp[s, :], lambda s: vp[s, :], (m_p, l_p, acc_p))
    for idx, d in enumerate(DILATIONS):
        if d == PAD_GROUP:
            branch(d, *padded_order, merge=False)
        else:
            branch(d, *token_order, merge=idx > 0)

    def out_body(c, carry):
        rows = pl.ds(pl.multiple_of(c * NORM_CHUNK, NORM_CHUNK), NORM_CHUNK)
        m_a, m_b = m_s[rows, :], from_padded(m_p, c)
        m_n = jnp.maximum(m_a, m_b)
        c_a = jnp.exp(m_a - m_n)
        c_b = jnp.exp(m_b - m_n)
        o_ref[0, 0, rows, :] = ((acc_s[rows, :] * c_a + from_padded(acc_p, c) * c_b)
                                / (l_s[rows, :] * c_a + from_padded(l_p, c) * c_b))
        return carry

    lax.fori_loop(0, n_chunk, out_body, 0)


def _attention(q, k, v, qg, kg):
    b, n_slab, s, _ = q.shape
    assert s % (ATTN_BLOCK * max(DILATIONS)) == 0 and s % NORM_CHUNK == 0 and max(DILATIONS) == PAD_GROUP
    slab_spec = pl.BlockSpec((1, 1, s, LANES), lambda i, j: (i, j, 0, 0))
    vec = pl.BlockSpec((1, LANES), lambda i, j: (0, 0))
    return pl.pallas_call(
        _attn_kernel,
        out_shape=jax.ShapeDtypeStruct(q.shape, F32),
        grid=(b, n_slab),
        in_specs=[slab_spec, slab_spec, slab_spec, vec, vec],
        out_specs=slab_spec,
        scratch_shapes=([pltpu.VMEM((s, LANES), F32) for _ in range(5)]
                        + [pltpu.VMEM((s // PAD_GROUP * PAD_PITCH, LANES), F32) for _ in range(6)]),
        compiler_params=_cparams("parallel", "parallel"),
        name="dilated_attn",
    )(q, k, v, qg, kg)


def _out_proj_kernel(x_ref, yl_ref, ya_ref, gl_ref, ga_ref, w_ref, gf_ref, wr_ref, br_ref,
                     xn_ref, h2_ref, lg_ref):
    d_lru = yl_ref.shape[2]
    nl = _rms(yl_ref[0], gl_ref[...]).astype(BF16)
    ya = jnp.concatenate([ya_ref[0, j] for j in range(ya_ref.shape[1])], axis=-1)
    na = _rms(ya, ga_ref[...]).astype(BF16)
    xn = x_ref[0] + (jnp.dot(nl, w_ref[0:d_lru, :], preferred_element_type=F32)
                     + jnp.dot(na, w_ref[d_lru:, :], preferred_element_type=F32))
    xn_ref[0] = xn
    h2 = _rms(xn, gf_ref[...])
    h2_ref[0] = h2
    lg_ref[0] = jnp.dot(h2, wr_ref[...], preferred_element_type=F32,
                        precision=lax.Precision.HIGHEST) + br_ref[...]


def _out_proj(x, yl, ya, gl, ga, w, gf, wr, br):
    b, s, d = x.shape
    d_lru = yl.shape[2]
    n_slab = ya.shape[1]
    tm = min(TM_PROJ, s)
    tok = pl.BlockSpec((1, tm, d), lambda i, j: (i, j, 0))
    const = lambda shape: pl.BlockSpec(shape, lambda i, j: (0,) * len(shape))
    return pl.pallas_call(
        _out_proj_kernel,
        out_shape=(jax.ShapeDtypeStruct((b, s, d), F32), jax.ShapeDtypeStruct((b, s, d), F32),
                   jax.ShapeDtypeStruct((b, s, ROUTER_LANES), F32)),
        grid=(b, s // tm),
        in_specs=[tok, pl.BlockSpec((1, tm, d_lru), lambda i, j: (i, j, 0)),
                  pl.BlockSpec((1, n_slab, tm, LANES), lambda i, j: (i, 0, j, 0)),
                  const((1, d_lru)), const((1, n_slab * LANES)), const(w.shape), const((1, d)),
                  const(wr.shape), const((1, ROUTER_LANES))],
        out_specs=(tok, tok, pl.BlockSpec((1, tm, ROUTER_LANES), lambda i, j: (i, j, 0))),
        compiler_params=_cparams("parallel", "parallel"),
        name="out_proj",
    )(x, yl, ya, gl, ga, w, gf, wr, br)


def _route_kernel(lg_ref, ids_ref, gw_ref, cnt_ref, carry):
    tr = lg_ref.shape[0]

    @pl.when(pl.program_id(0) == 0)
    def _():
        carry[...] = jnp.zeros_like(carry)

    lg = lg_ref[...]
    lane = lax.broadcasted_iota(jnp.int32, (tr, ROUTER_LANES), 1)
    big = jnp.int32(ROUTER_LANES)

    def argmax(vals):
        top = jnp.max(vals, axis=-1, keepdims=True)
        return top, jnp.min(jnp.where(vals == top, lane, big), axis=-1, keepdims=True)

    g_logit = jnp.where(lane < N_GROUPS, lg, -jnp.inf)
    g_top, g_idx = argmax(g_logit)
    p_top = 1.0 / jnp.sum(jnp.exp(g_logit - g_top), axis=-1, keepdims=True)
    e_lo = EXPERT_LANE0 + g_idx * EXPERTS_PER_GROUP
    e_logit = jnp.where((lane >= e_lo) & (lane < e_lo + EXPERTS_PER_GROUP), lg, -jnp.inf)
    v1, i1 = argmax(e_logit)
    v2, i2 = argmax(jnp.where(lane == i1, -jnp.inf, e_logit))
    e21 = jnp.exp(v2 - v1)
    w1 = 1.0 / (1.0 + e21) * p_top
    w2 = e21 / (1.0 + e21) * p_top

    hot = ((lane == i1) | (lane == i2))
    rr = lax.broadcasted_iota(jnp.int32, (tr, tr), 0)
    cc = lax.broadcasted_iota(jnp.int32, (tr, tr), 1)
    before = (rr > cc).astype(BF16)
    prefix = jnp.dot(before, hot.astype(BF16), preferred_element_type=F32) + carry[...]
    rank1 = jnp.sum(jnp.where(lane == i1, prefix, 0.0), axis=-1, keepdims=True).astype(jnp.int32)
    rank2 = jnp.sum(jnp.where(lane == i2, prefix, 0.0), axis=-1, keepdims=True).astype(jnp.int32)
    carry[...] = carry[...] + jnp.sum(hot.astype(F32), axis=0, keepdims=True)
    cnt_ref[...] = carry[...]

    ids_ref[...] = jnp.where(lane == 0, i1 - EXPERT_LANE0,
                             jnp.where(lane == 1, i2 - EXPERT_LANE0,
                                       jnp.where(lane == 2, rank1, jnp.where(lane == 3, rank2, 0))))
    gw_ref[...] = jnp.where(lane == 0, w1, jnp.where(lane == 1, w2, 0.0))


def _route(logits):
    t = logits.shape[0]
    tr = min(ROUTE_TILE, t)
    tile = pl.BlockSpec((tr, ROUTER_LANES), lambda i: (i, 0))
    return pl.pallas_call(
        _route_kernel,
        out_shape=(jax.ShapeDtypeStruct((t, ROUTER_LANES), jnp.int32),
                   jax.ShapeDtypeStruct((t, ROUTER_LANES), F32),
                   jax.ShapeDtypeStruct((1, ROUTER_LANES), F32)),
        grid=(t // tr,),
        in_specs=[tile],
        out_specs=(tile, tile, pl.BlockSpec((1, ROUTER_LANES), lambda i: (0, 0))),
        scratch_shapes=[pltpu.VMEM((1, ROUTER_LANES), F32)],
        compiler_params=_cparams("arbitrary"),
        name="route",
    )(logits)


def _row_copy(src_ref, src_row, dst_ref, dst_row, sem):
    return pltpu.make_async_copy(src_ref.at[pl.ds(src_row, 1)], dst_ref.at[pl.ds(dst_row, 1)], sem)


def _dispatch_kernel(dest_ref, h2_ref, zero_hbm, xbuf_hbm, sem):
    del zero_hbm
    n_tok = h2_ref.shape[0]

    def start(i, carry):
        for k in range(TOP_K):
            _row_copy(h2_ref, i, xbuf_hbm, dest_ref[TOP_K * i + k], sem).start(priority=k)
        return carry

    def wait(i, carry):
        for k in range(TOP_K):
            _row_copy(h2_ref, 0, xbuf_hbm, 0, sem).wait()
        return carry

    lax.fori_loop(0, n_tok, start, 0, unroll=DMA_UNROLL)
    lax.fori_loop(0, n_tok, wait, 0, unroll=DMA_UNROLL)


def _dispatch(dest, h2, n_rows):
    t, d = h2.shape
    td = min(DMA_TILE, t)
    return pl.pallas_call(
        _dispatch_kernel,
        out_shape=jax.ShapeDtypeStruct((n_rows, d), F32),
        grid=(t // td,),
        in_specs=[pl.BlockSpec((TOP_K * td,), lambda i: (i,), memory_space=pltpu.SMEM),
                  pl.BlockSpec((td, d), lambda i: (i, 0)), pl.BlockSpec(memory_space=pl.ANY)],
        out_specs=pl.BlockSpec(memory_space=pl.ANY),
        scratch_shapes=[pltpu.SemaphoreType.DMA],
        input_output_aliases={2: 0},
        compiler_params=_cparams("arbitrary"),
        name="dispatch",
    )(dest, h2, jnp.zeros((n_rows, d), F32))


def _expert_kernel(be_ref, x_ref, wg_ref, wu_ref, wd_ref, y_ref, wg_s, wu_s, wd_s):
    i = pl.program_id(0)
    new_expert = (i == 0) | (be_ref[i] != be_ref[jnp.maximum(i - 1, 0)])

    @pl.when(new_expert)
    def _():
        wg_s[...] = wg_ref[0, 0].astype(BF16)
        wu_s[...] = wu_ref[0, 0].astype(BF16)
        wd_s[...] = wd_ref[0, 0].astype(BF16)

    x = x_ref[...].astype(BF16)
    g = jnp.dot(x, wg_s[...], preferred_element_type=F32)
    u = jnp.dot(x, wu_s[...], preferred_element_type=F32)
    hid = (jax.nn.silu(g) * u).astype(BF16)
    y_ref[...] = jnp.dot(hid, wd_s[...], preferred_element_type=F32)


def _experts(blk_expert, xbuf, wg, wu, wd, layer):
    n_rows, d = xbuf.shape
    de = wg.shape[3]
    n_blk = n_rows // MOE_BLOCK
    rows = pl.BlockSpec((MOE_BLOCK, d), lambda i, be: (i, 0))
    return pl.pallas_call(
        _expert_kernel,
        out_shape=jax.ShapeDtypeStruct((n_rows, d), F32),
        grid_spec=pltpu.PrefetchScalarGridSpec(
            num_scalar_prefetch=1,
            grid=(n_blk,),
            in_specs=[rows,
                      pl.BlockSpec((1, 1, d, de), lambda i, be: (layer, be[i], 0, 0)),
                      pl.BlockSpec((1, 1, d, de), lambda i, be: (layer, be[i], 0, 0)),
                      pl.BlockSpec((1, 1, de, d), lambda i, be: (layer, be[i], 0, 0))],
            out_specs=rows,
            scratch_shapes=[pltpu.VMEM((d, de), BF16), pltpu.VMEM((d, de), BF16), pltpu.VMEM((de, d), BF16)]),
        compiler_params=_cparams("arbitrary"),
        name="experts",
    )(blk_expert, xbuf, wg, wu, wd)


def _combine_kernel(dest_ref, gw_ref, x_ref, ybuf_hbm, o_ref, rows, sem):
    n_tok = x_ref.shape[0]

    def start(i, carry):
        for k in range(TOP_K):
            _row_copy(ybuf_hbm, dest_ref[TOP_K * i + k], rows.at[k], i, sem).start(priority=k)
        return carry

    def wait(i, carry):
        for k in range(TOP_K):
            _row_copy(ybuf_hbm, 0, rows.at[k], 0, sem).wait()
        return carry

    lax.fori_loop(0, n_tok, start, 0, unroll=DMA_UNROLL)
    lax.fori_loop(0, n_tok, wait, 0, unroll=DMA_UNROLL)
    gw = gw_ref[...]
    o_ref[...] = x_ref[...] + (gw[:, 0:1] * rows[0] + gw[:, 1:2] * rows[1])


def _combine(dest, gw, x, ybuf):
    t, d = x.shape
    tc = min(DMA_TILE, t)
    tok = pl.BlockSpec((tc, d), lambda i: (i, 0))
    return pl.pallas_call(
        _combine_kernel,
        out_shape=jax.ShapeDtypeStruct((t, d), F32),
        grid=(t // tc,),
        in_specs=[pl.BlockSpec((TOP_K * tc,), lambda i: (i,), memory_space=pltpu.SMEM),
                  pl.BlockSpec((tc, ROUTER_LANES), lambda i: (i, 0)), tok,
                  pl.BlockSpec(memory_space=pl.ANY)],
        out_specs=tok,
        scratch_shapes=[pltpu.VMEM((TOP_K, tc, d), F32), pltpu.SemaphoreType.DMA],
        compiler_params=_cparams("arbitrary"),
        name="combine",
    )(dest, gw, x, ybuf)


def _block_diag(w):
    nb, bd, _ = w.shape
    eye = jnp.eye(nb, dtype=w.dtype)
    return (eye[:, None, :, None] * w[:, :, None, :]).reshape(nb * bd, nb * bd)


def _moe(x, h2, logits, wg, wu, wd, layer):
    t, d = x.shape
    ids, gw, cnt = _route(logits)
    counts = cnt[0, EXPERT_LANE0:EXPERT_LANE0 + N_EXPERTS].astype(jnp.int32)
    padded = (counts + MOE_BLOCK - 1) // MOE_BLOCK * MOE_BLOCK
    pend = jnp.cumsum(padded)
    pstart = pend - padded
    dest = (pstart[ids[:, 0:TOP_K]] + ids[:, TOP_K:2 * TOP_K]).reshape(t * TOP_K)
    n_rows = t * TOP_K + N_EXPERTS * MOE_BLOCK
    n_blk = n_rows // MOE_BLOCK
    blk_start = jnp.arange(n_blk, dtype=jnp.int32) * MOE_BLOCK
    blk_expert = jnp.minimum(jnp.sum(pend[None, :] <= blk_start[:, None], axis=1), N_EXPERTS - 1).astype(jnp.int32)
    xbuf = _dispatch(dest, h2, n_rows)
    ybuf = _experts(blk_expert, xbuf, wg, wu, wd, layer)
    return _combine(dest, gw, x, ybuf)


def kernel(x, norm_mix, w_in, conv_w, conv_b, lru_w_a, lru_b_a, lru_w_x, lru_b_x, lru_lambda, q_norm, k_norm, norm_out_lru, norm_out_attn, w_out, norm_ffn, router_group_w, router_group_b, router_expert_w, router_expert_b, w_gate, w_up, w_down):
    b, s, d = x.shape
    depth = w_in.shape[0]
    d_lru = conv_w.shape[2]
    d_attn = norm_out_attn.shape[1]
    pad = ROUTER_LANES - N_GROUPS - N_EXPERTS
    row = lambda v: v.reshape(1, -1)
    for l in range(depth):
        xl, gate, q, k, v = _in_proj(x, row(norm_mix[l]), w_in[l].astype(BF16), d_lru, d_attn)
        y_lru = _lru(xl, gate, conv_w[l], row(conv_b[l]),
                     _block_diag(lru_w_a[l]).astype(BF16), row(lru_b_a[l]),
                     _block_diag(lru_w_x[l]).astype(BF16), row(lru_b_x[l]), row(lru_lambda[l]))
        y_attn = _attention(q, k, v, row(jnp.tile(q_norm[l], HEADS_PER_SLAB)),
                            row(jnp.tile(k_norm[l], HEADS_PER_SLAB)))
        wr = jnp.pad(jnp.concatenate([router_group_w[l], router_expert_w[l]], axis=1), ((0, 0), (0, pad)))
        br = jnp.pad(jnp.concatenate([router_group_b[l], router_expert_b[l]]), (0, pad))
        xn, h2, logits = _out_proj(x, y_lru, y_attn, row(norm_out_lru[l]), row(norm_out_attn[l]),
                                   w_out[l].astype(BF16), row(norm_ffn[l]), wr, row(br))
        t = b * s
        x = _moe(xn.reshape(t, d), h2.reshape(t, d), logits.reshape(t, ROUTER_LANES),
                 w_gate, w_up, w_down, l).reshape(b, s, d)
    return x
```

```python
import functools

import jax
import jax.numpy as jnp
from jax import lax
from jax.experimental import pallas as pl
from jax.experimental.pallas import tpu as pltpu

F32 = jnp.float32
BF16 = jnp.bfloat16

EPS = 1e-6
LANES = 128
HEAD_DIM = 64
HEADS_PER_SLAB = LANES // HEAD_DIM
ATTN_BLOCK = 128
DILATIONS = (1, 4, 16)
CONV_WIDTH = 4
LRU_C = 8.0
N_GROUPS = 4
EXPERTS_PER_GROUP = 8
N_EXPERTS = N_GROUPS * EXPERTS_PER_GROUP
TOP_K = 2
ROUTER_LANES = LANES
EXPERT_LANE0 = N_GROUPS
VMEM_LIMIT = 56 * 1024 * 1024

TM_PROJ = 512
LRU_CHUNK = 256
PAD_GROUP = 16
PAD_PITCH = 24
ATTN_UNROLL = 4
NORM_CHUNK = 512
ROUTE_TILE = 512
MOE_BLOCK = 256
TOKEN_ROWS = 8
DMA_TILE = 1024
DMA_UNROLL = 8


def _rms(x, g):
    return x * lax.rsqrt(jnp.mean(x * x, axis=-1, keepdims=True) + EPS) * g


def _log1p(z):
    w = 1.0 + z
    return jnp.where(w == 1.0, z, z * jnp.log(w) / jnp.where(w == 1.0, 1.0, w - 1.0))


def _softplus(x):
    return jnp.maximum(x, 0.0) + _log1p(jnp.exp(-jnp.abs(x)))


def _one_minus_exp(x, exp_x):
    u = exp_x
    near = (1.0 - u) * x / jnp.log(jnp.where(u == 1.0, 0.5, u))
    near = jnp.where(u == 1.0, -x, near)
    return jnp.where(x > -0.5, near, 1.0 - u)


def _store_token_tiles(ref, val):
    n, d = val.shape
    r = d // LANES
    for j in range(r):
        ref[pl.ds(j, n, stride=r), :] = val[:, j * LANES:(j + 1) * LANES]


def _load_token_tiles(ref, n):
    r = ref.shape[0] // n
    return jnp.concatenate([ref[pl.ds(j, n, stride=r), :] for j in range(r)], axis=-1)


def _cparams(*sem):
    return pltpu.CompilerParams(dimension_semantics=sem, vmem_limit_bytes=VMEM_LIMIT)


def _in_proj_kernel(x_ref, g_ref, w_ref, xl_ref, gate_ref, q_ref, k_ref, v_ref, *, d_lru, d_attn):
    h = _rms(x_ref[0], g_ref[...]).astype(BF16)

    def cols(lo, n):
        return jnp.dot(h, w_ref[:, lo:lo + n], preferred_element_type=F32)

    xl_ref[0] = cols(0, d_lru)
    gate_ref[0] = cols(d_lru, d_lru)
    base = 2 * d_lru
    for ref in (q_ref, k_ref, v_ref):
        heads = cols(base, d_attn)
        for j in range(d_attn // LANES):
            ref[0, j] = heads[:, j * LANES:(j + 1) * LANES]
        base += d_attn


def _in_proj(x, g, w, d_lru, d_attn):
    b, s, d = x.shape
    tm = min(TM_PROJ, s)
    n_slab = d_attn // LANES
    slab = jax.ShapeDtypeStruct((b, n_slab, s, LANES), F32)
    row = jax.ShapeDtypeStruct((b, s, d_lru), F32)
    slab_spec = pl.BlockSpec((1, n_slab, tm, LANES), lambda i, j: (i, 0, j, 0))
    row_spec = pl.BlockSpec((1, tm, d_lru), lambda i, j: (i, j, 0))
    return pl.pallas_call(
        functools.partial(_in_proj_kernel, d_lru=d_lru, d_attn=d_attn),
        out_shape=(row, row, slab, slab, slab),
        grid=(b, s // tm),
        in_specs=[pl.BlockSpec((1, tm, d), lambda i, j: (i, j, 0)),
                  pl.BlockSpec((1, d), lambda i, j: (0, 0)),
                  pl.BlockSpec(w.shape, lambda i, j: (0, 0))],
        out_specs=(row_spec, row_spec, slab_spec, slab_spec, slab_spec),
        compiler_params=_cparams("parallel", "parallel"),
        name="in_proj",
    )(x, g, w)


def _lru_kernel(xl_ref, gate_ref, cw_ref, cb_ref, wa_ref, ba_ref, wx_ref, bx_ref, lam_ref,
                y_ref, xcat, h_scr):
    tc = xl_ref.shape[1]
    halo = 8

    @pl.when(pl.program_id(1) == 0)
    def _():
        xcat[0:halo, :] = jnp.zeros((halo, xcat.shape[1]), F32)
        h_scr[...] = jnp.zeros_like(h_scr)

    x = xl_ref[0]
    xcat[halo:halo + tc, :] = x
    xc = cb_ref[...]
    for j in range(CONV_WIDTH):
        xc = xc + cw_ref[j:j + 1, :] * xcat[pl.ds(halo - (CONV_WIDTH - 1) + j, tc), :]
    xcat[0:halo, :] = x[tc - halo:tc, :]

    xb = xc.astype(BF16)
    r = jax.nn.sigmoid(jnp.dot(xb, wa_ref[...], preferred_element_type=F32) + ba_ref[...])
    i = jax.nn.sigmoid(jnp.dot(xb, wx_ref[...], preferred_element_type=F32) + bx_ref[...])
    log_a = -LRU_C * r * _softplus(-lam_ref[...])
    a = jnp.exp(log_a)
    u = jnp.sqrt(_one_minus_exp(2.0 * log_a, a * a)) * (i * xc)

    row = lax.broadcasted_iota(jnp.int32, (tc, 1), 0)
    shift = 1
    while shift < tc:
        keep = row >= shift
        a_prev = pltpu.roll(a, shift, axis=0)
        u_prev = pltpu.roll(u, shift, axis=0)
        u = jnp.where(keep, a * u_prev + u, u)
        a = jnp.where(keep, a * a_prev, a)
        shift *= 2
    h = a * h_scr[...] + u
    h_scr[...] = h[tc - 1:tc, :]
    y_ref[0] = h * jax.nn.gelu(gate_ref[0])


def _lru(xl, gate, cw, cb, wa, ba, wx, bx, lam):
    b, s, c = xl.shape
    tc = min(LRU_CHUNK, s)
    row_spec = pl.BlockSpec((1, tc, c), lambda i, j: (i, j, 0))
    vec = pl.BlockSpec((1, c), lambda i, j: (0, 0))
    mat = pl.BlockSpec((c, c), lambda i, j: (0, 0))
    return pl.pallas_call(
        _lru_kernel,
        out_shape=jax.ShapeDtypeStruct((b, s, c), F32),
        grid=(b, s // tc),
        in_specs=[row_spec, row_spec, pl.BlockSpec((CONV_WIDTH, c), lambda i, j: (0, 0)), vec,
                  mat, vec, mat, vec, vec],
        out_specs=row_spec,
        scratch_shapes=[pltpu.VMEM((tc + 8, c), F32), pltpu.VMEM((1, c), F32)],
        compiler_params=_cparams("parallel", "arbitrary"),
        name="rg_lru",
    )(xl, gate, cw, cb, wa, ba, wx, bx, lam)


def _attn_kernel(q_ref, k_ref, v_ref, qg_ref, kg_ref, o_ref,
                 qn, kn, m_s, l_s, acc_s, qp, kp, vp, m_p, l_p, acc_p):
    s_len = q_ref.shape[2]
    blk = ATTN_BLOCK
    lane = lax.broadcasted_iota(jnp.int32, (1, LANES), 1)
    lo = lane < HEAD_DIM
    n_chunk = s_len // NORM_CHUNK
    groups = NORM_CHUNK // PAD_GROUP

    def padded_base(c):
        return pl.multiple_of(c * (groups * PAD_PITCH), 8)

    def to_padded(dst, c, val):
        for g in range(groups):
            dst[pl.ds(padded_base(c) + g * PAD_PITCH, PAD_GROUP), :] = val[g * PAD_GROUP:(g + 1) * PAD_GROUP]

    def from_padded(src, c):
        return jnp.concatenate(
            [src[pl.ds(padded_base(c) + g * PAD_PITCH, PAD_GROUP), :] for g in range(groups)], axis=0)

    def norm_body(c, carry):
        rows = pl.ds(pl.multiple_of(c * NORM_CHUNK, NORM_CHUNK), NORM_CHUNK)
        for src, g_ref, dst, dst_p, scale in ((q_ref, qg_ref, qn, qp, HEAD_DIM ** -0.5),
                                              (k_ref, kg_ref, kn, kp, 1.0)):
            x = src[0, 0, rows, :]
            x2 = x * x
            s0 = jnp.sum(jnp.where(lo, x2, 0.0), axis=-1, keepdims=True)
            s1 = jnp.sum(jnp.where(lo, 0.0, x2), axis=-1, keepdims=True)
            ms = jnp.where(lo, s0, s1) * (1.0 / HEAD_DIM)
            y = x * lax.rsqrt(ms + EPS) * g_ref[...] * scale
            dst[rows, :] = y
            to_padded(dst_p, c, y)
        to_padded(vp, c, v_ref[0, 0, rows, :])
        return carry

    lax.fori_loop(0, n_chunk, norm_body, 0)

    rows2 = lax.broadcasted_iota(jnp.int32, (HEADS_PER_SLAB * blk, blk), 0)
    qi = rows2 & (blk - 1)
    kj = lax.broadcasted_iota(jnp.int32, (HEADS_PER_SLAB * blk, blk), 1)
    band_prev = kj >= qi
    band_cur = kj <= qi
    nt = (((1,), (1,)), ((), ()))

    def branch(d, q_src, k_src, v_src, stats, merge):
        m_r, l_r, acc_r = stats
        log_d = d.bit_length() - 1
        padded = d == PAD_GROUP
        pitch = PAD_PITCH if padded else d
        span = blk * pitch

        def sl(start):
            return pl.ds(start, blk) if pitch == 1 else pl.ds(start, blk, stride=pitch)

        def body(it, carry):
            n = it >> log_d
            r = it & (d - 1)
            cur = sl(n * span + r)
            prev = sl(jnp.maximum(n - 1, 0) * span + r)
            qt = q_src(cur)
            qs = jnp.concatenate([jnp.where(lo, qt, 0.0), jnp.where(lo, 0.0, qt)], axis=0).astype(BF16)
            s_c = lax.dot_general(qs, k_src(cur).astype(BF16), nt, preferred_element_type=F32)
            s_p = lax.dot_general(qs, k_src(prev).astype(BF16), nt, preferred_element_type=F32)
            s_c = jnp.where(band_cur, s_c, -jnp.inf)
            s_p = jnp.where(band_prev & (n > 0), s_p, -jnp.inf)
            m = jnp.max(jnp.maximum(s_c, s_p), axis=-1, keepdims=True)
            p_c = jnp.exp(s_c - m)
            p_p = jnp.exp(s_p - m)
            l = jnp.sum(p_c + p_p, axis=-1, keepdims=True)
            acc = (jnp.dot(p_c.astype(BF16), v_src(cur).astype(BF16), preferred_element_type=F32)
                   + jnp.dot(p_p.astype(BF16), v_src(prev).astype(BF16), preferred_element_type=F32))
            acc_t = jnp.where(lo, acc[:blk], acc[blk:])
            m_t = jnp.where(lo, jnp.broadcast_to(m[:blk], (blk, LANES)), jnp.broadcast_to(m[blk:], (blk, LANES)))
            l_t = jnp.where(lo, jnp.broadcast_to(l[:blk], (blk, LANES)), jnp.broadcast_to(l[blk:], (blk, LANES)))
            if merge:
                m_o = m_r[cur, :]
                m_n = jnp.maximum(m_o, m_t)
                c_o = jnp.exp(m_o - m_n)
                c_t = jnp.exp(m_t - m_n)
                m_r[cur, :] = m_n
                l_r[cur, :] = l_r[cur, :] * c_o + l_t * c_t
                acc_r[cur, :] = acc_r[cur, :] * c_o + acc_t * c_t
            else:
                m_r[cur, :] = m_t
                l_r[cur, :] = l_t
                acc_r[cur, :] = acc_t
            return carry

        lax.fori_loop(0, s_len // blk, body, 0, unroll=ATTN_UNROLL)

    token_order = (lambda s: qn[s, :], lambda s: kn[s, :], lambda s: v_ref[0, 0, s, :], (m_s, l_s, acc_s))
    padded_order = (lambda s: qp[s, :], lambda s: kp[s, :], lambda s: vp[s, :], (m_p, l_p, acc_p))
    for idx, d in enumerate(DILATIONS):
        if d == PAD_GROUP:
            branch(d, *padded_order, merge=False)
        else:
            branch(d, *token_order, merge=idx > 0)

    def out_body(c, carry):
        rows = pl.ds(pl.multiple_of(c * NORM_CHUNK, NORM_CHUNK), NORM_CHUNK)
        m_a, m_b = m_s[rows, :], from_padded(m_p, c)
        m_n = jnp.maximum(m_a, m_b)
        c_a = jnp.exp(m_a - m_n)
        c_b = jnp.exp(m_b - m_n)
        o_ref[0, 0, rows, :] = ((acc_s[rows, :] * c_a + from_padded(acc_p, c) * c_b)
                                / (l_s[rows, :] * c_a + from_padded(l_p, c) * c_b))
        return carry

    lax.fori_loop(0, n_chunk, out_body, 0)


def _attention(q, k, v, qg, kg):
    b, n_slab, s, _ = q.shape
    assert s % (ATTN_BLOCK * max(DILATIONS)) == 0 and s % NORM_CHUNK == 0 and max(DILATIONS) == PAD_GROUP
    slab_spec = pl.BlockSpec((1, 1, s, LANES), lambda i, j: (i, j, 0, 0))
    vec = pl.BlockSpec((1, LANES), lambda i, j: (0, 0))
    return pl.pallas_call(
        _attn_kernel,
        out_shape=jax.ShapeDtypeStruct(q.shape, F32),
        grid=(b, n_slab),
        in_specs=[slab_spec, slab_spec, slab_spec, vec, vec],
        out_specs=slab_spec,
        scratch_shapes=([pltpu.VMEM((s, LANES), F32) for _ in range(5)]
                        + [pltpu.VMEM((s // PAD_GROUP * PAD_PITCH, LANES), F32) for _ in range(6)]),
        compiler_params=_cparams("parallel", "parallel"),
        name="dilated_attn",
    )(q, k, v, qg, kg)


def _out_proj_kernel(x_ref, yl_ref, ya_ref, gl_ref, ga_ref, w_ref, gf_ref, wr_ref, br_ref,
                     xn_ref, h2_ref, lg_ref):
    d_lru = yl_ref.shape[2]
    nl = _rms(yl_ref[0], gl_ref[...]).astype(BF16)
    ya = jnp.concatenate([ya_ref[0, j] for j in range(ya_ref.shape[1])], axis=-1)
    na = _rms(ya, ga_ref[...]).astype(BF16)
    xn = x_ref[0] + (jnp.dot(nl, w_ref[0:d_lru, :], preferred_element_type=F32)
                     + jnp.dot(na, w_ref[d_lru:, :], preferred_element_type=F32))
    xn_ref[0] = xn
    h2 = _rms(xn, gf_ref[...])
    _store_token_tiles(h2_ref, h2)
    hi = h2.astype(BF16)
    lo = (h2 - hi.astype(F32)).astype(BF16)
    both = jnp.dot(hi, wr_ref[...], preferred_element_type=F32)
    lg_ref[0] = (both[:, :ROUTER_LANES] + both[:, ROUTER_LANES:]
                 + jnp.dot(lo, wr_ref[:, :ROUTER_LANES], preferred_element_type=F32) + br_ref[...])


def _out_proj(x, yl, ya, gl, ga, w, gf, wr, br):
    b, s, d = x.shape
    d_lru = yl.shape[2]
    n_slab = ya.shape[1]
    tm = min(TM_PROJ, s)
    nj = s // tm
    rows_per_tok = d // LANES
    tok = pl.BlockSpec((1, tm, d), lambda i, j: (i, j, 0))
    tiles = pl.BlockSpec((tm * rows_per_tok, LANES), lambda i, j: (i * nj + j, 0))
    const = lambda shape: pl.BlockSpec(shape, lambda i, j: (0,) * len(shape))
    return pl.pallas_call(
        _out_proj_kernel,
        out_shape=(jax.ShapeDtypeStruct((b, s, d), F32),
                   jax.ShapeDtypeStruct((b * s * rows_per_tok, LANES), F32),
                   jax.ShapeDtypeStruct((b, s, ROUTER_LANES), F32)),
        grid=(b, s // tm),
        in_specs=[tok, pl.BlockSpec((1, tm, d_lru), lambda i, j: (i, j, 0)),
                  pl.BlockSpec((1, n_slab, tm, LANES), lambda i, j: (i, 0, j, 0)),
                  const((1, d_lru)), const((1, n_slab * LANES)), const(w.shape), const((1, d)),
                  const(wr.shape), const((1, ROUTER_LANES))],
        out_specs=(tok, tiles, pl.BlockSpec((1, tm, ROUTER_LANES), lambda i, j: (i, j, 0))),
        compiler_params=_cparams("parallel", "parallel"),
        name="out_proj",
    )(x, yl, ya, gl, ga, w, gf, wr, br)


def _route_kernel(lg_ref, ids_ref, gw_ref, cnt_ref, carry):
    tr = lg_ref.shape[0]

    @pl.when(pl.program_id(0) == 0)
    def _():
        carry[...] = jnp.zeros_like(carry)

    lg = lg_ref[...]
    lane = lax.broadcasted_iota(jnp.int32, (tr, ROUTER_LANES), 1)
    big = jnp.int32(ROUTER_LANES)

    def argmax(vals):
        top = jnp.max(vals, axis=-1, keepdims=True)
        return top, jnp.min(jnp.where(vals == top, lane, big), axis=-1, keepdims=True)

    g_logit = jnp.where(lane < N_GROUPS, lg, -jnp.inf)
    g_top, g_idx = argmax(g_logit)
    p_top = 1.0 / jnp.sum(jnp.exp(g_logit - g_top), axis=-1, keepdims=True)
    e_lo = EXPERT_LANE0 + g_idx * EXPERTS_PER_GROUP
    e_logit = jnp.where((lane >= e_lo) & (lane < e_lo + EXPERTS_PER_GROUP), lg, -jnp.inf)
    v1, i1 = argmax(e_logit)
    v2, i2 = argmax(jnp.where(lane == i1, -jnp.inf, e_logit))
    e21 = jnp.exp(v2 - v1)
    w1 = 1.0 / (1.0 + e21) * p_top
    w2 = e21 / (1.0 + e21) * p_top

    hot = ((lane == i1) | (lane == i2))
    rr = lax.broadcasted_iota(jnp.int32, (tr, tr), 0)
    cc = lax.broadcasted_iota(jnp.int32, (tr, tr), 1)
    before = (rr > cc).astype(BF16)
    prefix = jnp.dot(before, hot.astype(BF16), preferred_element_type=F32) + carry[...]
    rank1 = jnp.sum(jnp.where(lane == i1, prefix, 0.0), axis=-1, keepdims=True).astype(jnp.int32)
    rank2 = jnp.sum(jnp.where(lane == i2, prefix, 0.0), axis=-1, keepdims=True).astype(jnp.int32)
    carry[...] = carry[...] + jnp.sum(hot.astype(F32), axis=0, keepdims=True)
    cnt_ref[...] = carry[...]

    ids_ref[...] = jnp.where(lane == 0, i1 - EXPERT_LANE0,
                             jnp.where(lane == 1, i2 - EXPERT_LANE0,
                                       jnp.where(lane == 2, rank1, jnp.where(lane == 3, rank2, 0))))
    gw_ref[...] = jnp.where(lane == 0, w1, jnp.where(lane == 1, w2, 0.0))


def _route(logits):
    t = logits.shape[0]
    tr = min(ROUTE_TILE, t)
    tile = pl.BlockSpec((tr, ROUTER_LANES), lambda i: (i, 0))
    return pl.pallas_call(
        _route_kernel,
        out_shape=(jax.ShapeDtypeStruct((t, ROUTER_LANES), jnp.int32),
                   jax.ShapeDtypeStruct((t, ROUTER_LANES), F32),
                   jax.ShapeDtypeStruct((1, ROUTER_LANES), F32)),
        grid=(t // tr,),
        in_specs=[tile],
        out_specs=(tile, tile, pl.BlockSpec((1, ROUTER_LANES), lambda i: (0, 0))),
        scratch_shapes=[pltpu.VMEM((1, ROUTER_LANES), F32)],
        compiler_params=_cparams("arbitrary"),
        name="route",
    )(logits)


def _row_copy(src_ref, src_row, dst_ref, dst_row, sem):
    rows = lambda tok: pl.ds(pl.multiple_of(tok * TOKEN_ROWS, TOKEN_ROWS), TOKEN_ROWS)
    return pltpu.make_async_copy(src_ref.at[rows(src_row)], dst_ref.at[rows(dst_row)], sem)


def _dispatch_kernel(dest_ref, h2_ref, zero_hbm, xbuf_hbm, sem):
    del zero_hbm
    n_tok = h2_ref.shape[0] // TOKEN_ROWS

    def start(i, carry):
        for k in range(TOP_K):
            _row_copy(h2_ref, i, xbuf_hbm, dest_ref[TOP_K * i + k], sem).start(priority=k)
        return carry

    def wait(i, carry):
        for k in range(TOP_K):
            _row_copy(h2_ref, 0, xbuf_hbm, 0, sem).wait()
        return carry

    lax.fori_loop(0, n_tok, start, 0, unroll=DMA_UNROLL)
    lax.fori_loop(0, n_tok, wait, 0, unroll=DMA_UNROLL)


def _dispatch(dest, h2, n_rows):
    t = h2.shape[0] // TOKEN_ROWS
    td = min(DMA_TILE, t)
    return pl.pallas_call(
        _dispatch_kernel,
        out_shape=jax.ShapeDtypeStruct((n_rows * TOKEN_ROWS, LANES), F32),
        grid=(t // td,),
        in_specs=[pl.BlockSpec((TOP_K * td,), lambda i: (i,), memory_space=pltpu.SMEM),
                  pl.BlockSpec((td * TOKEN_ROWS, LANES), lambda i: (i, 0)), pl.BlockSpec(memory_space=pl.ANY)],
        out_specs=pl.BlockSpec(memory_space=pl.ANY),
        scratch_shapes=[pltpu.SemaphoreType.DMA],
        input_output_aliases={2: 0},
        compiler_params=_cparams("arbitrary"),
        name="dispatch",
    )(dest, h2, jnp.zeros((n_rows * TOKEN_ROWS, LANES), F32))


def _expert_kernel(be_ref, x_ref, wg_ref, wu_ref, wd_ref, y_ref, wg_s, wu_s, wd_s):
    i = pl.program_id(0)
    new_expert = (i == 0) | (be_ref[i] != be_ref[jnp.maximum(i - 1, 0)])

    @pl.when(new_expert)
    def _():
        wg_s[...] = wg_ref[0, 0].astype(BF16)
        wu_s[...] = wu_ref[0, 0].astype(BF16)
        wd_s[...] = wd_ref[0, 0].astype(BF16)

    x = _load_token_tiles(x_ref, MOE_BLOCK).astype(BF16)
    g = jnp.dot(x, wg_s[...], preferred_element_type=F32)
    u = jnp.dot(x, wu_s[...], preferred_element_type=F32)
    hid = (jax.nn.silu(g) * u).astype(BF16)
    _store_token_tiles(y_ref, jnp.dot(hid, wd_s[...], preferred_element_type=F32))


def _experts(blk_expert, xbuf, wg, wu, wd, layer):
    d, de = wg.shape[2], wg.shape[3]
    assert d == TOKEN_ROWS * LANES
    n_blk = xbuf.shape[0] // (MOE_BLOCK * TOKEN_ROWS)
    rows = pl.BlockSpec((MOE_BLOCK * TOKEN_ROWS, LANES), lambda i, be: (i, 0))
    return pl.pallas_call(
        _expert_kernel,
        out_shape=jax.ShapeDtypeStruct(xbuf.shape, F32),
        grid_spec=pltpu.PrefetchScalarGridSpec(
            num_scalar_prefetch=1,
            grid=(n_blk,),
            in_specs=[rows,
                      pl.BlockSpec((1, 1, d, de), lambda i, be: (layer, be[i], 0, 0)),
                      pl.BlockSpec((1, 1, d, de), lambda i, be: (layer, be[i], 0, 0)),
                      pl.BlockSpec((1, 1, de, d), lambda i, be: (layer, be[i], 0, 0))],
            out_specs=rows,
            scratch_shapes=[pltpu.VMEM((d, de), BF16), pltpu.VMEM((d, de), BF16), pltpu.VMEM((de, d), BF16)]),
        compiler_params=_cparams("arbitrary"),
        name="experts",
    )(blk_expert, xbuf, wg, wu, wd)


def _combine_kernel(dest_ref, gw_ref, x_ref, ybuf_hbm, o_ref, rows0, rows1, sem):
    n_tok = x_ref.shape[0]
    rows = (rows0, rows1)

    def start(i, carry):
        for k in range(TOP_K):
            _row_copy(ybuf_hbm, dest_ref[TOP_K * i + k], rows[k], i, sem).start(priority=k)
        return carry

    def wait(i, carry):
        for k in range(TOP_K):
            _row_copy(ybuf_hbm, 0, rows[k], 0, sem).wait()
        return carry

    lax.fori_loop(0, n_tok, start, 0, unroll=DMA_UNROLL)
    lax.fori_loop(0, n_tok, wait, 0, unroll=DMA_UNROLL)
    gw = gw_ref[...]
    o_ref[...] = x_ref[...] + (gw[:, 0:1] * _load_token_tiles(rows0, n_tok)
                               + gw[:, 1:2] * _load_token_tiles(rows1, n_tok))


def _combine(dest, gw, x, ybuf):
    t, d = x.shape
    tc = min(DMA_TILE, t)
    tok = pl.BlockSpec((tc, d), lambda i: (i, 0))
    return pl.pallas_call(
        _combine_kernel,
        out_shape=jax.ShapeDtypeStruct((t, d), F32),
        grid=(t // tc,),
        in_specs=[pl.BlockSpec((TOP_K * tc,), lambda i: (i,), memory_space=pltpu.SMEM),
                  pl.BlockSpec((tc, ROUTER_LANES), lambda i: (i, 0)), tok,
                  pl.BlockSpec(memory_space=pl.ANY)],
        out_specs=tok,
        scratch_shapes=[pltpu.VMEM((tc * TOKEN_ROWS, LANES), F32), pltpu.VMEM((tc * TOKEN_ROWS, LANES), F32),
                        pltpu.SemaphoreType.DMA],
        compiler_params=_cparams("arbitrary"),
        name="combine",
    )(dest, gw, x, ybuf)


def _block_diag(w):
    nb, bd, _ = w.shape
    eye = jnp.eye(nb, dtype=w.dtype)
    return (eye[:, None, :, None] * w[:, :, None, :]).reshape(nb * bd, nb * bd)


def _moe(x, h2, logits, wg, wu, wd, layer):
    t, d = x.shape
    ids, gw, cnt = _route(logits)
    counts = cnt[0, EXPERT_LANE0:EXPERT_LANE0 + N_EXPERTS].astype(jnp.int32)
    padded = (counts + MOE_BLOCK - 1) // MOE_BLOCK * MOE_BLOCK
    pend = jnp.cumsum(padded)
    pstart = pend - padded
    dest = (pstart[ids[:, 0:TOP_K]] + ids[:, TOP_K:2 * TOP_K]).reshape(t * TOP_K)
    n_rows = t * TOP_K + N_EXPERTS * MOE_BLOCK
    n_blk = n_rows // MOE_BLOCK
    blk_start = jnp.arange(n_blk, dtype=jnp.int32) * MOE_BLOCK
    blk_expert = jnp.minimum(jnp.sum(pend[None, :] <= blk_start[:, None], axis=1), N_EXPERTS - 1).astype(jnp.int32)
    xbuf = _dispatch(dest, h2, n_rows)
    ybuf = _experts(blk_expert, xbuf, wg, wu, wd, layer)
    return _combine(dest, gw, x, ybuf)


def kernel(x, norm_mix, w_in, conv_w, conv_b, lru_w_a, lru_b_a, lru_w_x, lru_b_x, lru_lambda, q_norm, k_norm, norm_out_lru, norm_out_attn, w_out, norm_ffn, router_group_w, router_group_b, router_expert_w, router_expert_b, w_gate, w_up, w_down):
    b, s, d = x.shape
    depth = w_in.shape[0]
    d_lru = conv_w.shape[2]
    d_attn = norm_out_attn.shape[1]
    pad = ROUTER_LANES - N_GROUPS - N_EXPERTS
    row = lambda v: v.reshape(1, -1)
    for l in range(depth):
        xl, gate, q, k, v = _in_proj(x, row(norm_mix[l]), w_in[l].astype(BF16), d_lru, d_attn)
        y_lru = _lru(xl, gate, conv_w[l], row(conv_b[l]),
                     _block_diag(lru_w_a[l]).astype(BF16), row(lru_b_a[l]),
                     _block_diag(lru_w_x[l]).astype(BF16), row(lru_b_x[l]), row(lru_lambda[l]))
        y_attn = _attention(q, k, v, row(jnp.tile(q_norm[l], HEADS_PER_SLAB)),
                            row(jnp.tile(k_norm[l], HEADS_PER_SLAB)))
        wr = jnp.pad(jnp.concatenate([router_group_w[l], router_expert_w[l]], axis=1), ((0, 0), (0, pad)))
        wr_hi = wr.astype(BF16)
        wr = jnp.concatenate([wr_hi, (wr - wr_hi.astype(F32)).astype(BF16)], axis=1)
        br = jnp.pad(jnp.concatenate([router_group_b[l], router_expert_b[l]]), (0, pad))
        xn, h2, logits = _out_proj(x, y_lru, y_attn, row(norm_out_lru[l]), row(norm_out_attn[l]),
                                   w_out[l].astype(BF16), row(norm_ffn[l]), wr, row(br))
        t = b * s
        x = _moe(xn.reshape(t, d), h2, logits.reshape(t, ROUTER_LANES),
                 w_gate, w_up, w_down, l).reshape(b, s, d)
    return x
```

```python
import functools

import jax
import jax.numpy as jnp
from jax import lax
from jax.experimental import pallas as pl
from jax.experimental.pallas import tpu as pltpu

F32 = jnp.float32
BF16 = jnp.bfloat16

EPS = 1e-6
LANES = 128
HEAD_DIM = 64
HEADS_PER_SLAB = LANES // HEAD_DIM
ATTN_BLOCK = 128
DILATIONS = (1, 4, 16)
CONV_WIDTH = 4
LRU_C = 8.0
N_GROUPS = 4
EXPERTS_PER_GROUP = 8
N_EXPERTS = N_GROUPS * EXPERTS_PER_GROUP
TOP_K = 2
ROUTER_LANES = LANES
EXPERT_LANE0 = N_GROUPS
VMEM_LIMIT = 56 * 1024 * 1024

TM_PROJ = 512
LRU_CHUNK = 256
SCAN_GROUP = 8
PAD_GROUP = 16
PAD_PITCH = 24
ATTN_UNROLL = 8
NORM_CHUNK = 512
ROUTE_TILE = 512
MOE_BLOCK = 256
TOKEN_ROWS = 8
DMA_TILE = 1024
DMA_UNROLL = 8


def _rms(x, g):
    return x * lax.rsqrt(jnp.mean(x * x, axis=-1, keepdims=True) + EPS) * g


def _log1p(z):
    w = 1.0 + z
    return jnp.where(w == 1.0, z, z * jnp.log(w) / jnp.where(w == 1.0, 1.0, w - 1.0))


def _softplus(x):
    return jnp.maximum(x, 0.0) + _log1p(jnp.exp(-jnp.abs(x)))


def _store_token_tiles(ref, val):
    n, d = val.shape
    r = d // LANES
    for j in range(r):
        ref[pl.ds(j, n, stride=r), :] = val[:, j * LANES:(j + 1) * LANES]


def _load_token_tiles(ref, n):
    r = ref.shape[0] // n
    return jnp.concatenate([ref[pl.ds(j, n, stride=r), :] for j in range(r)], axis=-1)


def _cparams(*sem):
    return pltpu.CompilerParams(dimension_semantics=sem, vmem_limit_bytes=VMEM_LIMIT)


def _in_proj_kernel(x_ref, g_ref, w_ref, xl_ref, gate_ref, q_ref, k_ref, v_ref, *, d_lru, d_attn):
    h = _rms(x_ref[0], g_ref[...]).astype(BF16)

    def cols(lo, n):
        return jnp.dot(h, w_ref[:, lo:lo + n], preferred_element_type=F32)

    xl_ref[0] = cols(0, d_lru)
    gate_ref[0] = cols(d_lru, d_lru)
    base = 2 * d_lru
    for ref in (q_ref, k_ref, v_ref):
        heads = cols(base, d_attn)
        for j in range(d_attn // LANES):
            ref[0, j] = heads[:, j * LANES:(j + 1) * LANES]
        base += d_attn


def _in_proj(x, g, w, d_lru, d_attn):
    b, s, d = x.shape
    tm = min(TM_PROJ, s)
    n_slab = d_attn // LANES
    slab = jax.ShapeDtypeStruct((b, n_slab, s, LANES), F32)
    row = jax.ShapeDtypeStruct((b, s, d_lru), F32)
    slab_spec = pl.BlockSpec((1, n_slab, tm, LANES), lambda i, j: (i, 0, j, 0))
    row_spec = pl.BlockSpec((1, tm, d_lru), lambda i, j: (i, j, 0))
    return pl.pallas_call(
        functools.partial(_in_proj_kernel, d_lru=d_lru, d_attn=d_attn),
        out_shape=(row, row, slab, slab, slab),
        grid=(b, s // tm),
        in_specs=[pl.BlockSpec((1, tm, d), lambda i, j: (i, j, 0)),
                  pl.BlockSpec((1, d), lambda i, j: (0, 0)),
                  pl.BlockSpec(w.shape, lambda i, j: (0, 0))],
        out_specs=(row_spec, row_spec, slab_spec, slab_spec, slab_spec),
        compiler_params=_cparams("parallel", "parallel"),
        name="in_proj",
    )(x, g, w)


def _lru_kernel(xl_ref, gate_ref, cw_ref, cb_ref, wa_ref, ba_ref, wx_ref, bx_ref, lam_ref,
                y_ref, tail_scr, h_scr):
    tc = xl_ref.shape[1]
    halo = 8

    @pl.when(pl.program_id(1) == 0)
    def _():
        tail_scr[...] = jnp.zeros_like(tail_scr)
        h_scr[...] = jnp.zeros_like(h_scr)

    x = xl_ref[0]
    tail = tail_scr[...]
    first = lax.broadcasted_iota(jnp.int32, (halo, 1), 0)
    xc = cb_ref[...]
    for j in range(CONV_WIDTH):
        back = CONV_WIDTH - 1 - j
        if back == 0:
            xs = x
        else:
            shifted = pltpu.roll(x, back, axis=0)
            head = jnp.where(first < back, pltpu.roll(tail, back, axis=0), shifted[0:halo])
            xs = jnp.concatenate([head, shifted[halo:]], axis=0)
        xc = xc + cw_ref[j:j + 1, :] * xs
    tail_scr[...] = x[tc - halo:tc, :]

    xb = xc.astype(BF16)
    r = jax.nn.sigmoid(jnp.dot(xb, wa_ref[...], preferred_element_type=F32) + ba_ref[...])
    i = jax.nn.sigmoid(jnp.dot(xb, wx_ref[...], preferred_element_type=F32) + bx_ref[...])
    log_a = -LRU_C * r * _softplus(-lam_ref[...])
    a = jnp.exp(log_a)
    u = jnp.sqrt(jnp.tanh(-log_a) * (1.0 + a * a)) * (i * xc)

    in_group = lax.broadcasted_iota(jnp.int32, (tc, 1), 0) & (SCAN_GROUP - 1)
    shift = 1
    while shift < SCAN_GROUP:
        keep = in_group >= shift
        a_prev = pltpu.roll(a, shift, axis=0)
        u_prev = pltpu.roll(u, shift, axis=0)
        u = jnp.where(keep, a * u_prev + u, u)
        a = jnp.where(keep, a * a_prev, a)
        shift *= 2
    h_last = h_scr[...]
    groups = []
    for g in range(tc // SCAN_GROUP):
        rows = slice(g * SCAN_GROUP, (g + 1) * SCAN_GROUP)
        h_g = a[rows] * h_last + u[rows]
        h_last = h_g[SCAN_GROUP - 1:SCAN_GROUP, :]
        groups.append(h_g)
    h_scr[...] = h_last
    y_ref[0] = jnp.concatenate(groups, axis=0) * jax.nn.gelu(gate_ref[0])


def _lru(xl, gate, cw, cb, wa, ba, wx, bx, lam):
    b, s, c = xl.shape
    tc = min(LRU_CHUNK, s)
    row_spec = pl.BlockSpec((1, tc, c), lambda i, j: (i, j, 0))
    vec = pl.BlockSpec((1, c), lambda i, j: (0, 0))
    mat = pl.BlockSpec((c, c), lambda i, j: (0, 0))
    return pl.pallas_call(
        _lru_kernel,
        out_shape=jax.ShapeDtypeStruct((b, s, c), F32),
        grid=(b, s // tc),
        in_specs=[row_spec, row_spec, pl.BlockSpec((CONV_WIDTH, c), lambda i, j: (0, 0)), vec,
                  mat, vec, mat, vec, vec],
        out_specs=row_spec,
        scratch_shapes=[pltpu.VMEM((8, c), F32), pltpu.VMEM((1, c), F32)],
        compiler_params=_cparams("parallel", "arbitrary"),
        name="rg_lru",
    )(xl, gate, cw, cb, wa, ba, wx, bx, lam)


def _attn_kernel(q_ref, k_ref, v_ref, qg_ref, kg_ref, o_ref,
                 qn, kn, m_s, l_s, acc_s, qp, kp, vp, m_p, l_p, acc_p):
    s_len = q_ref.shape[2]
    blk = ATTN_BLOCK
    lane = lax.broadcasted_iota(jnp.int32, (1, LANES), 1)
    lo = lane < HEAD_DIM
    n_chunk = s_len // NORM_CHUNK
    groups = NORM_CHUNK // PAD_GROUP

    def padded_base(c):
        return pl.multiple_of(c * (groups * PAD_PITCH), 8)

    def to_padded(dst, c, val):
        for g in range(groups):
            dst[pl.ds(padded_base(c) + g * PAD_PITCH, PAD_GROUP), :] = val[g * PAD_GROUP:(g + 1) * PAD_GROUP]

    def from_padded(src, c):
        return jnp.concatenate(
            [src[pl.ds(padded_base(c) + g * PAD_PITCH, PAD_GROUP), :] for g in range(groups)], axis=0)

    def norm_body(c, carry):
        rows = pl.ds(pl.multiple_of(c * NORM_CHUNK, NORM_CHUNK), NORM_CHUNK)
        for src, g_ref, dst, dst_p, scale in ((q_ref, qg_ref, qn, qp, HEAD_DIM ** -0.5),
                                              (k_ref, kg_ref, kn, kp, 1.0)):
            x = src[0, 0, rows, :]
            x2 = x * x
            s0 = jnp.sum(jnp.where(lo, x2, 0.0), axis=-1, keepdims=True)
            s1 = jnp.sum(jnp.where(lo, 0.0, x2), axis=-1, keepdims=True)
            ms = jnp.where(lo, s0, s1) * (1.0 / HEAD_DIM)
            y = x * lax.rsqrt(ms + EPS) * g_ref[...] * scale
            dst[rows, :] = y
            to_padded(dst_p, c, y)
        to_padded(vp, c, v_ref[0, 0, rows, :])
        return carry

    lax.fori_loop(0, n_chunk, norm_body, 0)

    rows2 = lax.broadcasted_iota(jnp.int32, (HEADS_PER_SLAB * blk, blk), 0)
    qi = rows2 & (blk - 1)
    kj = lax.broadcasted_iota(jnp.int32, (HEADS_PER_SLAB * blk, blk), 1)
    band_prev = kj >= qi
    band_cur = kj <= qi
    nt = (((1,), (1,)), ((), ()))

    def branch(d, q_src, k_src, v_src, stats, merge):
        m_r, l_r, acc_r = stats
        log_d = d.bit_length() - 1
        padded = d == PAD_GROUP
        pitch = PAD_PITCH if padded else d
        span = blk * pitch

        def sl(start):
            return pl.ds(start, blk) if pitch == 1 else pl.ds(start, blk, stride=pitch)

        def body(it, carry):
            n = it >> log_d
            r = it & (d - 1)
            cur = sl(n * span + r)
            prev = sl(jnp.maximum(n - 1, 0) * span + r)
            qt = q_src(cur)
            qs = jnp.concatenate([jnp.where(lo, qt, 0.0), jnp.where(lo, 0.0, qt)], axis=0).astype(BF16)
            s_c = lax.dot_general(qs, k_src(cur).astype(BF16), nt, preferred_element_type=F32)
            s_p = lax.dot_general(qs, k_src(prev).astype(BF16), nt, preferred_element_type=F32)
            s_c = jnp.where(band_cur, s_c, -jnp.inf)
            s_p = jnp.where(band_prev & (n > 0), s_p, -jnp.inf)
            m = jnp.max(jnp.maximum(s_c, s_p), axis=-1, keepdims=True)
            p_c = jnp.exp(s_c - m)
            p_p = jnp.exp(s_p - m)
            l = jnp.sum(p_c + p_p, axis=-1, keepdims=True)
            acc = (jnp.dot(p_c.astype(BF16), v_src(cur).astype(BF16), preferred_element_type=F32)
                   + jnp.dot(p_p.astype(BF16), v_src(prev).astype(BF16), preferred_element_type=F32))
            acc_t = jnp.where(lo, acc[:blk], acc[blk:])
            m_t = jnp.where(lo, jnp.broadcast_to(m[:blk], (blk, LANES)), jnp.broadcast_to(m[blk:], (blk, LANES)))
            l_t = jnp.where(lo, jnp.broadcast_to(l[:blk], (blk, LANES)), jnp.broadcast_to(l[blk:], (blk, LANES)))
            if merge:
                m_o = m_r[cur, :]
                m_n = jnp.maximum(m_o, m_t)
                c_o = jnp.exp(m_o - m_n)
                c_t = jnp.exp(m_t - m_n)
                m_r[cur, :] = m_n
                l_r[cur, :] = l_r[cur, :] * c_o + l_t * c_t
                acc_r[cur, :] = acc_r[cur, :] * c_o + acc_t * c_t
            else:
                m_r[cur, :] = m_t
                l_r[cur, :] = l_t
                acc_r[cur, :] = acc_t
            return carry

        lax.fori_loop(0, s_len // blk, body, 0, unroll=ATTN_UNROLL)

    token_order = (lambda s: qn[s, :], lambda s: kn[s, :], lambda s: v_ref[0, 0, s, :], (m_s, l_s, acc_s))
    padded_order = (lambda s: qp[s, :], lambda s: kp[s, :], lambda s: vp[s, :], (m_p, l_p, acc_p))
    for idx, d in enumerate(DILATIONS):
        if d == PAD_GROUP:
            branch(d, *padded_order, merge=False)
        else:
            branch(d, *token_order, merge=idx > 0)

    def out_body(c, carry):
        rows = pl.ds(pl.multiple_of(c * NORM_CHUNK, NORM_CHUNK), NORM_CHUNK)
        m_a, m_b = m_s[rows, :], from_padded(m_p, c)
        m_n = jnp.maximum(m_a, m_b)
        c_a = jnp.exp(m_a - m_n)
        c_b = jnp.exp(m_b - m_n)
        o_ref[0, 0, rows, :] = ((acc_s[rows, :] * c_a + from_padded(acc_p, c) * c_b)
                                / (l_s[rows, :] * c_a + from_padded(l_p, c) * c_b))
        return carry

    lax.fori_loop(0, n_chunk, out_body, 0)


def _attention(q, k, v, qg, kg):
    b, n_slab, s, _ = q.shape
    assert s % (ATTN_BLOCK * max(DILATIONS)) == 0 and s % NORM_CHUNK == 0 and max(DILATIONS) == PAD_GROUP
    slab_spec = pl.BlockSpec((1, 1, s, LANES), lambda i, j: (i, j, 0, 0))
    vec = pl.BlockSpec((1, LANES), lambda i, j: (0, 0))
    return pl.pallas_call(
        _attn_kernel,
        out_shape=jax.ShapeDtypeStruct(q.shape, F32),
        grid=(b, n_slab),
        in_specs=[slab_spec, slab_spec, slab_spec, vec, vec],
        out_specs=slab_spec,
        scratch_shapes=([pltpu.VMEM((s, LANES), F32) for _ in range(5)]
                        + [pltpu.VMEM((s // PAD_GROUP * PAD_PITCH, LANES), F32) for _ in range(6)]),
        compiler_params=_cparams("parallel", "parallel"),
        name="dilated_attn",
    )(q, k, v, qg, kg)


def _out_proj_kernel(x_ref, yl_ref, ya_ref, gl_ref, ga_ref, w_ref, gf_ref, wr_ref, br_ref,
                     xn_ref, h2_ref, lg_ref):
    d_lru = yl_ref.shape[2]
    nl = _rms(yl_ref[0], gl_ref[...]).astype(BF16)
    ya = jnp.concatenate([ya_ref[0, j] for j in range(ya_ref.shape[1])], axis=-1)
    na = _rms(ya, ga_ref[...]).astype(BF16)
    xn = x_ref[0] + (jnp.dot(nl, w_ref[0:d_lru, :], preferred_element_type=F32)
                     + jnp.dot(na, w_ref[d_lru:, :], preferred_element_type=F32))
    xn_ref[0] = xn
    h2 = _rms(xn, gf_ref[...])
    _store_token_tiles(h2_ref, h2)
    hi = h2.astype(BF16)
    lo = (h2 - hi.astype(F32)).astype(BF16)
    both = jnp.dot(hi, wr_ref[...], preferred_element_type=F32)
    lg_ref[0] = (both[:, :ROUTER_LANES] + both[:, ROUTER_LANES:]
                 + jnp.dot(lo, wr_ref[:, :ROUTER_LANES], preferred_element_type=F32) + br_ref[...])


def _out_proj(x, yl, ya, gl, ga, w, gf, wr, br):
    b, s, d = x.shape
    d_lru = yl.shape[2]
    n_slab = ya.shape[1]
    tm = min(TM_PROJ, s)
    nj = s // tm
    rows_per_tok = d // LANES
    tok = pl.BlockSpec((1, tm, d), lambda i, j: (i, j, 0))
    tiles = pl.BlockSpec((tm * rows_per_tok, LANES), lambda i, j: (i * nj + j, 0))
    const = lambda shape: pl.BlockSpec(shape, lambda i, j: (0,) * len(shape))
    return pl.pallas_call(
        _out_proj_kernel,
        out_shape=(jax.ShapeDtypeStruct((b, s, d), F32),
                   jax.ShapeDtypeStruct((b * s * rows_per_tok, LANES), F32),
                   jax.ShapeDtypeStruct((b, s, ROUTER_LANES), F32)),
        grid=(b, s // tm),
        in_specs=[tok, pl.BlockSpec((1, tm, d_lru), lambda i, j: (i, j, 0)),
                  pl.BlockSpec((1, n_slab, tm, LANES), lambda i, j: (i, 0, j, 0)),
                  const((1, d_lru)), const((1, n_slab * LANES)), const(w.shape), const((1, d)),
                  const(wr.shape), const((1, ROUTER_LANES))],
        out_specs=(tok, tiles, pl.BlockSpec((1, tm, ROUTER_LANES), lambda i, j: (i, j, 0))),
        compiler_params=_cparams("parallel", "parallel"),
        name="out_proj",
    )(x, yl, ya, gl, ga, w, gf, wr, br)


def _route_kernel(lg_ref, ids_ref, gw_ref, cnt_ref, carry):
    tr = lg_ref.shape[0]

    @pl.when(pl.program_id(0) == 0)
    def _():
        carry[...] = jnp.zeros_like(carry)

    lg = lg_ref[...]
    lane = lax.broadcasted_iota(jnp.int32, (tr, ROUTER_LANES), 1)
    big = jnp.int32(ROUTER_LANES)

    def argmax(vals):
        top = jnp.max(vals, axis=-1, keepdims=True)
        return top, jnp.min(jnp.where(vals == top, lane, big), axis=-1, keepdims=True)

    g_logit = jnp.where(lane < N_GROUPS, lg, -jnp.inf)
    g_top, g_idx = argmax(g_logit)
    p_top = 1.0 / jnp.sum(jnp.exp(g_logit - g_top), axis=-1, keepdims=True)
    e_lo = EXPERT_LANE0 + g_idx * EXPERTS_PER_GROUP
    e_logit = jnp.where((lane >= e_lo) & (lane < e_lo + EXPERTS_PER_GROUP), lg, -jnp.inf)
    v1, i1 = argmax(e_logit)
    v2, i2 = argmax(jnp.where(lane == i1, -jnp.inf, e_logit))
    e21 = jnp.exp(v2 - v1)
    w1 = 1.0 / (1.0 + e21) * p_top
    w2 = e21 / (1.0 + e21) * p_top

    hot = ((lane == i1) | (lane == i2))
    rr = lax.broadcasted_iota(jnp.int32, (tr, tr), 0)
    cc = lax.broadcasted_iota(jnp.int32, (tr, tr), 1)
    before = (rr > cc).astype(BF16)
    prefix = jnp.dot(before, hot.astype(BF16), preferred_element_type=F32) + carry[...]
    rank1 = jnp.sum(jnp.where(lane == i1, prefix, 0.0), axis=-1, keepdims=True).astype(jnp.int32)
    rank2 = jnp.sum(jnp.where(lane == i2, prefix, 0.0), axis=-1, keepdims=True).astype(jnp.int32)
    carry[...] = carry[...] + jnp.sum(hot.astype(F32), axis=0, keepdims=True)
    cnt_ref[...] = carry[...]

    ids_ref[...] = jnp.where(lane == 0, i1 - EXPERT_LANE0,
                             jnp.where(lane == 1, i2 - EXPERT_LANE0,
                                       jnp.where(lane == 2, rank1, jnp.where(lane == 3, rank2, 0))))
    gw_ref[...] = jnp.where(lane == 0, w1, jnp.where(lane == 1, w2, 0.0))


def _route(logits):
    t = logits.shape[0]
    tr = min(ROUTE_TILE, t)
    tile = pl.BlockSpec((tr, ROUTER_LANES), lambda i: (i, 0))
    return pl.pallas_call(
        _route_kernel,
        out_shape=(jax.ShapeDtypeStruct((t, ROUTER_LANES), jnp.int32),
                   jax.ShapeDtypeStruct((t, ROUTER_LANES), F32),
                   jax.ShapeDtypeStruct((1, ROUTER_LANES), F32)),
        grid=(t // tr,),
        in_specs=[tile],
        out_specs=(tile, tile, pl.BlockSpec((1, ROUTER_LANES), lambda i: (0, 0))),
        scratch_shapes=[pltpu.VMEM((1, ROUTER_LANES), F32)],
        compiler_params=_cparams("arbitrary"),
        name="route",
    )(logits)


def _row_copy(src_ref, src_row, dst_ref, dst_row, sem):
    rows = lambda tok: pl.ds(pl.multiple_of(tok * TOKEN_ROWS, TOKEN_ROWS), TOKEN_ROWS)
    return pltpu.make_async_copy(src_ref.at[rows(src_row)], dst_ref.at[rows(dst_row)], sem)


def _dispatch_kernel(dest_ref, pad_ref, h2_ref, xbuf_hbm, zero_s, sem):
    n_tok = h2_ref.shape[0] // TOKEN_ROWS
    n_pad = pad_ref.shape[0]
    zero_s[...] = jnp.zeros_like(zero_s)

    def start(i, carry):
        for k in range(TOP_K):
            _row_copy(h2_ref, i, xbuf_hbm, dest_ref[TOP_K * i + k], sem).start(priority=k)
        return carry

    def start_pad(i, carry):
        _row_copy(zero_s, 0, xbuf_hbm, pad_ref[i], sem).start()
        return carry

    def wait(i, carry):
        _row_copy(h2_ref, 0, xbuf_hbm, 0, sem).wait()
        return carry

    lax.fori_loop(0, n_tok, start, 0, unroll=DMA_UNROLL)
    lax.fori_loop(0, n_pad, start_pad, 0, unroll=DMA_UNROLL)
    lax.fori_loop(0, TOP_K * n_tok + n_pad, wait, 0, unroll=DMA_UNROLL)


def _dispatch(dest, pad_dest, h2, n_rows):
    t = h2.shape[0] // TOKEN_ROWS
    td = min(DMA_TILE, t)
    steps = t // td
    assert pad_dest.shape[0] % steps == 0 and t * TOP_K + pad_dest.shape[0] == n_rows
    return pl.pallas_call(
        _dispatch_kernel,
        out_shape=jax.ShapeDtypeStruct((n_rows * TOKEN_ROWS, LANES), F32),
        grid=(steps,),
        in_specs=[pl.BlockSpec((TOP_K * td,), lambda i: (i,), memory_space=pltpu.SMEM),
                  pl.BlockSpec((pad_dest.shape[0] // steps,), lambda i: (i,), memory_space=pltpu.SMEM),
                  pl.BlockSpec((td * TOKEN_ROWS, LANES), lambda i: (i, 0))],
        out_specs=pl.BlockSpec(memory_space=pl.ANY),
        scratch_shapes=[pltpu.VMEM((TOKEN_ROWS, LANES), F32), pltpu.SemaphoreType.DMA],
        compiler_params=_cparams("arbitrary"),
        name="dispatch",
    )(dest, pad_dest, h2)


def _expert_kernel(be_ref, used_ref, x_ref, wg_ref, wu_ref, wd_ref, y_ref, wg_s, wu_s, wd_s):
    i = pl.program_id(0)
    new_expert = (i == 0) | (be_ref[i] != be_ref[jnp.maximum(i - 1, 0)])
    in_use = i < used_ref[0]

    @pl.when(new_expert & in_use)
    def _():
        wg_s[...] = wg_ref[0, 0].astype(BF16)
        wu_s[...] = wu_ref[0, 0].astype(BF16)
        wd_s[...] = wd_ref[0, 0].astype(BF16)

    @pl.when(in_use)
    def _():
        x = _load_token_tiles(x_ref, MOE_BLOCK).astype(BF16)
        g = jnp.dot(x, wg_s[...], preferred_element_type=F32)
        u = jnp.dot(x, wu_s[...], preferred_element_type=F32)
        hid = (jax.nn.silu(g) * u).astype(BF16)
        _store_token_tiles(y_ref, jnp.dot(hid, wd_s[...], preferred_element_type=F32))

    @pl.when(jnp.logical_not(in_use))
    def _():
        y_ref[...] = jnp.zeros_like(y_ref)


def _experts(blk_expert, blocks_used, xbuf, wg, wu, wd, layer):
    d, de = wg.shape[2], wg.shape[3]
    assert d == TOKEN_ROWS * LANES
    n_blk = xbuf.shape[0] // (MOE_BLOCK * TOKEN_ROWS)
    rows = pl.BlockSpec((MOE_BLOCK * TOKEN_ROWS, LANES), lambda i, be, used: (i, 0))
    w_in = pl.BlockSpec((1, 1, d, de), lambda i, be, used: (layer, be[i], 0, 0))
    w_out = pl.BlockSpec((1, 1, de, d), lambda i, be, used: (layer, be[i], 0, 0))
    return pl.pallas_call(
        _expert_kernel,
        out_shape=jax.ShapeDtypeStruct(xbuf.shape, F32),
        grid_spec=pltpu.PrefetchScalarGridSpec(
            num_scalar_prefetch=2,
            grid=(n_blk,),
            in_specs=[rows, w_in, w_in, w_out],
            out_specs=rows,
            scratch_shapes=[pltpu.VMEM((d, de), BF16), pltpu.VMEM((d, de), BF16), pltpu.VMEM((de, d), BF16)]),
        compiler_params=_cparams("arbitrary"),
        name="experts",
    )(blk_expert, blocks_used, xbuf, wg, wu, wd)


def _combine_kernel(dest_ref, gw_ref, x_ref, ybuf_hbm, o_ref, rows0, rows1, sem):
    n_tok = x_ref.shape[0]
    rows = (rows0, rows1)

    def start(i, carry):
        for k in range(TOP_K):
            _row_copy(ybuf_hbm, dest_ref[TOP_K * i + k], rows[k], i, sem).start(priority=k)
        return carry

    def wait(i, carry):
        for k in range(TOP_K):
            _row_copy(ybuf_hbm, 0, rows[k], 0, sem).wait()
        return carry

    lax.fori_loop(0, n_tok, start, 0, unroll=DMA_UNROLL)
    lax.fori_loop(0, n_tok, wait, 0, unroll=DMA_UNROLL)
    gw = gw_ref[...]
    o_ref[...] = x_ref[...] + (gw[:, 0:1] * _load_token_tiles(rows0, n_tok)
                               + gw[:, 1:2] * _load_token_tiles(rows1, n_tok))


def _combine(dest, gw, x, ybuf):
    t, d = x.shape
    tc = min(DMA_TILE, t)
    tok = pl.BlockSpec((tc, d), lambda i: (i, 0))
    return pl.pallas_call(
        _combine_kernel,
        out_shape=jax.ShapeDtypeStruct((t, d), F32),
        grid=(t // tc,),
        in_specs=[pl.BlockSpec((TOP_K * tc,), lambda i: (i,), memory_space=pltpu.SMEM),
                  pl.BlockSpec((tc, ROUTER_LANES), lambda i: (i, 0)), tok,
                  pl.BlockSpec(memory_space=pl.ANY)],
        out_specs=tok,
        scratch_shapes=[pltpu.VMEM((tc * TOKEN_ROWS, LANES), F32), pltpu.VMEM((tc * TOKEN_ROWS, LANES), F32),
                        pltpu.SemaphoreType.DMA],
        compiler_params=_cparams("arbitrary"),
        name="combine",
    )(dest, gw, x, ybuf)


def _block_diag(w):
    nb, bd, _ = w.shape
    eye = jnp.eye(nb, dtype=w.dtype)
    return (eye[:, None, :, None] * w[:, :, None, :]).reshape(nb * bd, nb * bd)


def _moe(x, h2, logits, wg, wu, wd, layer):
    t, d = x.shape
    ids, gw, cnt = _route(logits)
    counts = cnt[0, EXPERT_LANE0:EXPERT_LANE0 + N_EXPERTS].astype(jnp.int32)
    padded = (counts + MOE_BLOCK - 1) // MOE_BLOCK * MOE_BLOCK
    pend = jnp.cumsum(padded)
    pstart = pend - padded
    dest = (pstart[ids[:, 0:TOP_K]] + ids[:, TOP_K:2 * TOP_K]).reshape(t * TOP_K)
    n_rows = t * TOP_K + N_EXPERTS * MOE_BLOCK
    n_blk = n_rows // MOE_BLOCK
    blk_start = jnp.arange(n_blk, dtype=jnp.int32) * MOE_BLOCK
    blk_expert = jnp.minimum(jnp.sum(pend[None, :] <= blk_start[:, None], axis=1), N_EXPERTS - 1).astype(jnp.int32)
    n_pad = N_EXPERTS * MOE_BLOCK
    pad_cum = jnp.cumsum(padded - counts)
    seg_first = jnp.concatenate([pstart + counts, pend[-1:]])
    seg_skip = jnp.concatenate([jnp.zeros((1,), jnp.int32), pad_cum])
    j = jnp.arange(n_pad, dtype=jnp.int32)
    seg = jnp.sum(pad_cum[None, :] <= j[:, None], axis=1)
    pad_dest = (seg_first[seg] + (j - seg_skip[seg])).astype(jnp.int32)
    blocks_used = (pend[-1:] // MOE_BLOCK).astype(jnp.int32)
    xbuf = _dispatch(dest, pad_dest, h2, n_rows)
    ybuf = _experts(blk_expert, blocks_used, xbuf, wg, wu, wd, layer)
    return _combine(dest, gw, x, ybuf)


def kernel(x, norm_mix, w_in, conv_w, conv_b, lru_w_a, lru_b_a, lru_w_x, lru_b_x, lru_lambda, q_norm, k_norm, norm_out_lru, norm_out_attn, w_out, norm_ffn, router_group_w, router_group_b, router_expert_w, router_expert_b, w_gate, w_up, w_down):
    b, s, d = x.shape
    depth = w_in.shape[0]
    d_lru = conv_w.shape[2]
    d_attn = norm_out_attn.shape[1]
    pad = ROUTER_LANES - N_GROUPS - N_EXPERTS
    row = lambda v: v.reshape(1, -1)
    for l in range(depth):
        xl, gate, q, k, v = _in_proj(x, row(norm_mix[l]), w_in[l].astype(BF16), d_lru, d_attn)
        y_lru = _lru(xl, gate, conv_w[l], row(conv_b[l]),
                     _block_diag(lru_w_a[l]).astype(BF16), row(lru_b_a[l]),
                     _block_diag(lru_w_x[l]).astype(BF16), row(lru_b_x[l]), row(lru_lambda[l]))
        y_attn = _attention(q, k, v, row(jnp.tile(q_norm[l], HEADS_PER_SLAB)),
                            row(jnp.tile(k_norm[l], HEADS_PER_SLAB)))
        wr = jnp.pad(jnp.concatenate([router_group_w[l], router_expert_w[l]], axis=1), ((0, 0), (0, pad)))
        wr_hi = wr.astype(BF16)
        wr = jnp.concatenate([wr_hi, (wr - wr_hi.astype(F32)).astype(BF16)], axis=1)
        br = jnp.pad(jnp.concatenate([router_group_b[l], router_expert_b[l]]), (0, pad))
        xn, h2, logits = _out_proj(x, y_lru, y_attn, row(norm_out_lru[l]), row(norm_out_attn[l]),
                                   w_out[l].astype(BF16), row(norm_ffn[l]), wr, row(br))
        t = b * s
        x = _moe(xn.reshape(t, d), h2, logits.reshape(t, ROUTER_LANES),
                 w_gate, w_up, w_down, l).reshape(b, s, d)
    return x
```

```python
import functools

import jax
import jax.numpy as jnp
from jax import lax
from jax.experimental import pallas as pl
from jax.experimental.pallas import tpu as pltpu

F32 = jnp.float32
BF16 = jnp.bfloat16

EPS = 1e-6
LANES = 128
HEAD_DIM = 64
HEADS_PER_SLAB = LANES // HEAD_DIM
ATTN_BLOCK = 128
DILATIONS = (1, 4, 16)
CONV_WIDTH = 4
LRU_C = 8.0
N_GROUPS = 4
EXPERTS_PER_GROUP = 8
N_EXPERTS = N_GROUPS * EXPERTS_PER_GROUP
TOP_K = 2
ROUTER_LANES = LANES
EXPERT_LANE0 = N_GROUPS
VMEM_LIMIT = 56 * 1024 * 1024

TM_PROJ = 512
LRU_CHUNK = 256
SCAN_GROUP = 8
PAD_GROUP = 16
PAD_PITCH = 24
ATTN_UNROLL = 8
NORM_CHUNK = 512
ROUTE_TILE = 512
MOE_BLOCK = 256
TOKEN_ROWS = 4
DMA_TILE = 1024
DMA_UNROLL = 8


def _rms(x, g):
    return x * lax.rsqrt(jnp.mean(x * x, axis=-1, keepdims=True) + EPS) * g


def _log1p(z):
    w = 1.0 + z
    return jnp.where(w == 1.0, z, z * jnp.log(w) / jnp.where(w == 1.0, 1.0, w - 1.0))


def _softplus(x):
    return jnp.maximum(x, 0.0) + _log1p(jnp.exp(-jnp.abs(x)))


def _store_token_tiles(ref, val):
    n, d = val.shape
    assert d == 2 * TOKEN_ROWS * LANES
    for r in range(TOKEN_ROWS):
        hi = val[:, r * LANES:(r + 1) * LANES]
        lo = val[:, d // 2 + r * LANES:d // 2 + (r + 1) * LANES]
        hi = lax.bitcast_convert_type(hi.astype(BF16).astype(F32), jnp.uint32)
        lo = lax.bitcast_convert_type(lo.astype(BF16).astype(F32), jnp.uint32)
        ref[pl.ds(r, n, stride=TOKEN_ROWS), :] = hi | (lo >> 16)


def _load_token_tiles(ref, n):
    words = [ref[pl.ds(r, n, stride=TOKEN_ROWS), :] for r in range(TOKEN_ROWS)]
    hi = [lax.bitcast_convert_type(w & jnp.uint32(0xFFFF0000), F32) for w in words]
    lo = [lax.bitcast_convert_type(w << 16, F32) for w in words]
    return jnp.concatenate(hi + lo, axis=-1)


def _cparams(*sem):
    return pltpu.CompilerParams(dimension_semantics=sem, vmem_limit_bytes=VMEM_LIMIT)


def _in_proj_kernel(x_ref, g_ref, w_ref, xl_ref, gate_ref, q_ref, k_ref, v_ref, *, d_lru, d_attn):
    h = _rms(x_ref[0], g_ref[...]).astype(BF16)

    def cols(lo, n):
        return jnp.dot(h, w_ref[:, lo:lo + n], preferred_element_type=F32)

    xl_ref[0] = cols(0, d_lru)
    gate_ref[0] = cols(d_lru, d_lru)
    base = 2 * d_lru
    for ref in (q_ref, k_ref, v_ref):
        heads = cols(base, d_attn)
        for j in range(d_attn // LANES):
            ref[0, j] = heads[:, j * LANES:(j + 1) * LANES]
        base += d_attn


def _in_proj(x, g, w, d_lru, d_attn):
    b, s, d = x.shape
    tm = min(TM_PROJ, s)
    n_slab = d_attn // LANES
    slab = jax.ShapeDtypeStruct((b, n_slab, s, LANES), F32)
    row = jax.ShapeDtypeStruct((b, s, d_lru), F32)
    slab_spec = pl.BlockSpec((1, n_slab, tm, LANES), lambda i, j: (i, 0, j, 0))
    row_spec = pl.BlockSpec((1, tm, d_lru), lambda i, j: (i, j, 0))
    return pl.pallas_call(
        functools.partial(_in_proj_kernel, d_lru=d_lru, d_attn=d_attn),
        out_shape=(row, row, slab, slab, slab),
        grid=(b, s // tm),
        in_specs=[pl.BlockSpec((1, tm, d), lambda i, j: (i, j, 0)),
                  pl.BlockSpec((1, d), lambda i, j: (0, 0)),
                  pl.BlockSpec(w.shape, lambda i, j: (0, 0))],
        out_specs=(row_spec, row_spec, slab_spec, slab_spec, slab_spec),
        compiler_params=_cparams("parallel", "parallel"),
        name="in_proj",
    )(x, g, w)


def _lru_kernel(xl_ref, gate_ref, cw_ref, cb_ref, wa_ref, ba_ref, wx_ref, bx_ref, lam_ref,
                y_ref, tail_scr, h_scr):
    tc = xl_ref.shape[1]
    halo = 8

    @pl.when(pl.program_id(1) == 0)
    def _():
        tail_scr[...] = jnp.zeros_like(tail_scr)
        h_scr[...] = jnp.zeros_like(h_scr)

    x = xl_ref[0]
    tail = tail_scr[...]
    first = lax.broadcasted_iota(jnp.int32, (halo, 1), 0)
    xc = cb_ref[...]
    for j in range(CONV_WIDTH):
        back = CONV_WIDTH - 1 - j
        if back == 0:
            xs = x
        else:
            shifted = pltpu.roll(x, back, axis=0)
            head = jnp.where(first < back, pltpu.roll(tail, back, axis=0), shifted[0:halo])
            xs = jnp.concatenate([head, shifted[halo:]], axis=0)
        xc = xc + cw_ref[j:j + 1, :] * xs
    tail_scr[...] = x[tc - halo:tc, :]

    xb = xc.astype(BF16)
    r = jax.nn.sigmoid(jnp.dot(xb, wa_ref[...], preferred_element_type=F32) + ba_ref[...])
    i = jax.nn.sigmoid(jnp.dot(xb, wx_ref[...], preferred_element_type=F32) + bx_ref[...])
    log_a = -LRU_C * r * _softplus(-lam_ref[...])
    a = jnp.exp(log_a)
    u = jnp.sqrt(jnp.tanh(-log_a) * (1.0 + a * a)) * (i * xc)

    in_group = lax.broadcasted_iota(jnp.int32, (tc, 1), 0) & (SCAN_GROUP - 1)
    shift = 1
    while shift < SCAN_GROUP:
        keep = in_group >= shift
        a_prev = pltpu.roll(a, shift, axis=0)
        u_prev = pltpu.roll(u, shift, axis=0)
        u = jnp.where(keep, a * u_prev + u, u)
        a = jnp.where(keep, a * a_prev, a)
        shift *= 2
    h_last = h_scr[...]
    groups = []
    for g in range(tc // SCAN_GROUP):
        rows = slice(g * SCAN_GROUP, (g + 1) * SCAN_GROUP)
        h_g = a[rows] * h_last + u[rows]
        h_last = h_g[SCAN_GROUP - 1:SCAN_GROUP, :]
        groups.append(h_g)
    h_scr[...] = h_last
    y_ref[0] = jnp.concatenate(groups, axis=0) * jax.nn.gelu(gate_ref[0])


def _lru(xl, gate, cw, cb, wa, ba, wx, bx, lam):
    b, s, c = xl.shape
    tc = min(LRU_CHUNK, s)
    row_spec = pl.BlockSpec((1, tc, c), lambda i, j: (i, j, 0))
    vec = pl.BlockSpec((1, c), lambda i, j: (0, 0))
    mat = pl.BlockSpec((c, c), lambda i, j: (0, 0))
    return pl.pallas_call(
        _lru_kernel,
        out_shape=jax.ShapeDtypeStruct((b, s, c), F32),
        grid=(b, s // tc),
        in_specs=[row_spec, row_spec, pl.BlockSpec((CONV_WIDTH, c), lambda i, j: (0, 0)), vec,
                  mat, vec, mat, vec, vec],
        out_specs=row_spec,
        scratch_shapes=[pltpu.VMEM((8, c), F32), pltpu.VMEM((1, c), F32)],
        compiler_params=_cparams("parallel", "arbitrary"),
        name="rg_lru",
    )(xl, gate, cw, cb, wa, ba, wx, bx, lam)


def _attn_kernel(q_ref, k_ref, v_ref, qg_ref, kg_ref, o_ref,
                 qn, kn, m_s, l_s, acc_s, qp, kp, vp, m_p, l_p, acc_p):
    s_len = q_ref.shape[2]
    blk = ATTN_BLOCK
    lane = lax.broadcasted_iota(jnp.int32, (1, LANES), 1)
    lo = lane < HEAD_DIM
    n_chunk = s_len // NORM_CHUNK
    groups = NORM_CHUNK // PAD_GROUP

    def padded_base(c):
        return pl.multiple_of(c * (groups * PAD_PITCH), 8)

    def to_padded(dst, c, val):
        for g in range(groups):
            dst[pl.ds(padded_base(c) + g * PAD_PITCH, PAD_GROUP), :] = val[g * PAD_GROUP:(g + 1) * PAD_GROUP]

    def from_padded(src, c):
        return jnp.concatenate(
            [src[pl.ds(padded_base(c) + g * PAD_PITCH, PAD_GROUP), :] for g in range(groups)], axis=0)

    def norm_body(c, carry):
        rows = pl.ds(pl.multiple_of(c * NORM_CHUNK, NORM_CHUNK), NORM_CHUNK)
        for src, g_ref, dst, dst_p, scale in ((q_ref, qg_ref, qn, qp, HEAD_DIM ** -0.5),
                                              (k_ref, kg_ref, kn, kp, 1.0)):
            x = src[0, 0, rows, :]
            x2 = x * x
            s0 = jnp.sum(jnp.where(lo, x2, 0.0), axis=-1, keepdims=True)
            s1 = jnp.sum(jnp.where(lo, 0.0, x2), axis=-1, keepdims=True)
            ms = jnp.where(lo, s0, s1) * (1.0 / HEAD_DIM)
            y = x * lax.rsqrt(ms + EPS) * g_ref[...] * scale
            dst[rows, :] = y
            to_padded(dst_p, c, y)
        to_padded(vp, c, v_ref[0, 0, rows, :])
        return carry

    lax.fori_loop(0, n_chunk, norm_body, 0)

    rows2 = lax.broadcasted_iota(jnp.int32, (HEADS_PER_SLAB * blk, blk), 0)
    qi = rows2 & (blk - 1)
    kj = lax.broadcasted_iota(jnp.int32, (HEADS_PER_SLAB * blk, blk), 1)
    band_prev = kj >= qi
    band_cur = kj <= qi
    nt = (((1,), (1,)), ((), ()))

    def branch(d, q_src, k_src, v_src, stats, merge):
        m_r, l_r, acc_r = stats
        log_d = d.bit_length() - 1
        padded = d == PAD_GROUP
        pitch = PAD_PITCH if padded else d
        span = blk * pitch

        def sl(start):
            return pl.ds(start, blk) if pitch == 1 else pl.ds(start, blk, stride=pitch)

        def body(it, carry):
            n = it >> log_d
            r = it & (d - 1)
            cur = sl(n * span + r)
            prev = sl(jnp.maximum(n - 1, 0) * span + r)
            qt = q_src(cur)
            qs = jnp.concatenate([jnp.where(lo, qt, 0.0), jnp.where(lo, 0.0, qt)], axis=0).astype(BF16)
            s_c = lax.dot_general(qs, k_src(cur).astype(BF16), nt, preferred_element_type=F32)
            s_p = lax.dot_general(qs, k_src(prev).astype(BF16), nt, preferred_element_type=F32)
            s_c = jnp.where(band_cur, s_c, -jnp.inf)
            s_p = jnp.where(band_prev & (n > 0), s_p, -jnp.inf)
            m = jnp.max(jnp.maximum(s_c, s_p), axis=-1, keepdims=True)
            p_c = jnp.exp(s_c - m)
            p_p = jnp.exp(s_p - m)
            l = jnp.sum(p_c + p_p, axis=-1, keepdims=True)
            acc = (jnp.dot(p_c.astype(BF16), v_src(cur).astype(BF16), preferred_element_type=F32)
                   + jnp.dot(p_p.astype(BF16), v_src(prev).astype(BF16), preferred_element_type=F32))
            acc_t = jnp.where(lo, acc[:blk], acc[blk:])
            m_t = jnp.where(lo, jnp.broadcast_to(m[:blk], (blk, LANES)), jnp.broadcast_to(m[blk:], (blk, LANES)))
            l_t = jnp.where(lo, jnp.broadcast_to(l[:blk], (blk, LANES)), jnp.broadcast_to(l[blk:], (blk, LANES)))
            if merge:
                m_o = m_r[cur, :]
                m_n = jnp.maximum(m_o, m_t)
                c_o = jnp.exp(m_o - m_n)
                c_t = jnp.exp(m_t - m_n)
                m_r[cur, :] = m_n
                l_r[cur, :] = l_r[cur, :] * c_o + l_t * c_t
                acc_r[cur, :] = acc_r[cur, :] * c_o + acc_t * c_t
            else:
                m_r[cur, :] = m_t
                l_r[cur, :] = l_t
                acc_r[cur, :] = acc_t
            return carry

        lax.fori_loop(0, s_len // blk, body, 0, unroll=ATTN_UNROLL)

    token_order = (lambda s: qn[s, :], lambda s: kn[s, :], lambda s: v_ref[0, 0, s, :], (m_s, l_s, acc_s))
    padded_order = (lambda s: qp[s, :], lambda s: kp[s, :], lambda s: vp[s, :], (m_p, l_p, acc_p))
    for idx, d in enumerate(DILATIONS):
        if d == PAD_GROUP:
            branch(d, *padded_order, merge=False)
        else:
            branch(d, *token_order, merge=idx > 0)

    def out_body(c, carry):
        rows = pl.ds(pl.multiple_of(c * NORM_CHUNK, NORM_CHUNK), NORM_CHUNK)
        m_a, m_b = m_s[rows, :], from_padded(m_p, c)
        m_n = jnp.maximum(m_a, m_b)
        c_a = jnp.exp(m_a - m_n)
        c_b = jnp.exp(m_b - m_n)
        o_ref[0, 0, rows, :] = ((acc_s[rows, :] * c_a + from_padded(acc_p, c) * c_b)
                                / (l_s[rows, :] * c_a + from_padded(l_p, c) * c_b))
        return carry

    lax.fori_loop(0, n_chunk, out_body, 0)


def _attention(q, k, v, qg, kg):
    b, n_slab, s, _ = q.shape
    assert s % (ATTN_BLOCK * max(DILATIONS)) == 0 and s % NORM_CHUNK == 0 and max(DILATIONS) == PAD_GROUP
    slab_spec = pl.BlockSpec((1, 1, s, LANES), lambda i, j: (i, j, 0, 0))
    vec = pl.BlockSpec((1, LANES), lambda i, j: (0, 0))
    return pl.pallas_call(
        _attn_kernel,
        out_shape=jax.ShapeDtypeStruct(q.shape, F32),
        grid=(b, n_slab),
        in_specs=[slab_spec, slab_spec, slab_spec, vec, vec],
        out_specs=slab_spec,
        scratch_shapes=([pltpu.VMEM((s, LANES), F32) for _ in range(5)]
                        + [pltpu.VMEM((s // PAD_GROUP * PAD_PITCH, LANES), F32) for _ in range(6)]),
        compiler_params=_cparams("parallel", "parallel"),
        name="dilated_attn",
    )(q, k, v, qg, kg)


def _out_proj_kernel(x_ref, yl_ref, ya_ref, gl_ref, ga_ref, w_ref, gf_ref, wr_ref, br_ref,
                     xn_ref, h2_ref, lg_ref):
    d_lru = yl_ref.shape[2]
    nl = _rms(yl_ref[0], gl_ref[...]).astype(BF16)
    ya = jnp.concatenate([ya_ref[0, j] for j in range(ya_ref.shape[1])], axis=-1)
    na = _rms(ya, ga_ref[...]).astype(BF16)
    xn = x_ref[0] + (jnp.dot(nl, w_ref[0:d_lru, :], preferred_element_type=F32)
                     + jnp.dot(na, w_ref[d_lru:, :], preferred_element_type=F32))
    xn_ref[0] = xn
    h2 = _rms(xn, gf_ref[...])
    _store_token_tiles(h2_ref, h2)
    hi = h2.astype(BF16)
    lo = (h2 - hi.astype(F32)).astype(BF16)
    both = jnp.dot(hi, wr_ref[...], preferred_element_type=F32)
    lg_ref[0] = (both[:, :ROUTER_LANES] + both[:, ROUTER_LANES:]
                 + jnp.dot(lo, wr_ref[:, :ROUTER_LANES], preferred_element_type=F32) + br_ref[...])


def _out_proj(x, yl, ya, gl, ga, w, gf, wr, br):
    b, s, d = x.shape
    d_lru = yl.shape[2]
    n_slab = ya.shape[1]
    tm = min(TM_PROJ, s)
    nj = s // tm
    tok = pl.BlockSpec((1, tm, d), lambda i, j: (i, j, 0))
    tiles = pl.BlockSpec((tm * TOKEN_ROWS, LANES), lambda i, j: (i * nj + j, 0))
    const = lambda shape: pl.BlockSpec(shape, lambda i, j: (0,) * len(shape))
    return pl.pallas_call(
        _out_proj_kernel,
        out_shape=(jax.ShapeDtypeStruct((b, s, d), F32),
                   jax.ShapeDtypeStruct((b * s * TOKEN_ROWS, LANES), jnp.uint32),
                   jax.ShapeDtypeStruct((b, s, ROUTER_LANES), F32)),
        grid=(b, s // tm),
        in_specs=[tok, pl.BlockSpec((1, tm, d_lru), lambda i, j: (i, j, 0)),
                  pl.BlockSpec((1, n_slab, tm, LANES), lambda i, j: (i, 0, j, 0)),
                  const((1, d_lru)), const((1, n_slab * LANES)), const(w.shape), const((1, d)),
                  const(wr.shape), const((1, ROUTER_LANES))],
        out_specs=(tok, tiles, pl.BlockSpec((1, tm, ROUTER_LANES), lambda i, j: (i, j, 0))),
        compiler_params=_cparams("parallel", "parallel"),
        name="out_proj",
    )(x, yl, ya, gl, ga, w, gf, wr, br)


def _route_kernel(lg_ref, ids_ref, gw_ref, cnt_ref, carry):
    tr = lg_ref.shape[0]

    @pl.when(pl.program_id(0) == 0)
    def _():
        carry[...] = jnp.zeros_like(carry)

    lg = lg_ref[...]
    lane = lax.broadcasted_iota(jnp.int32, (tr, ROUTER_LANES), 1)
    big = jnp.int32(ROUTER_LANES)

    def argmax(vals):
        top = jnp.max(vals, axis=-1, keepdims=True)
        return top, jnp.min(jnp.where(vals == top, lane, big), axis=-1, keepdims=True)

    g_logit = jnp.where(lane < N_GROUPS, lg, -jnp.inf)
    g_top, g_idx = argmax(g_logit)
    p_top = 1.0 / jnp.sum(jnp.exp(g_logit - g_top), axis=-1, keepdims=True)
    e_lo = EXPERT_LANE0 + g_idx * EXPERTS_PER_GROUP
    e_logit = jnp.where((lane >= e_lo) & (lane < e_lo + EXPERTS_PER_GROUP), lg, -jnp.inf)
    v1, i1 = argmax(e_logit)
    v2, i2 = argmax(jnp.where(lane == i1, -jnp.inf, e_logit))
    e21 = jnp.exp(v2 - v1)
    w1 = 1.0 / (1.0 + e21) * p_top
    w2 = e21 / (1.0 + e21) * p_top

    hot = ((lane == i1) | (lane == i2))
    rr = lax.broadcasted_iota(jnp.int32, (tr, tr), 0)
    cc = lax.broadcasted_iota(jnp.int32, (tr, tr), 1)
    before = (rr > cc).astype(BF16)
    prefix = jnp.dot(before, hot.astype(BF16), preferred_element_type=F32) + carry[...]
    rank1 = jnp.sum(jnp.where(lane == i1, prefix, 0.0), axis=-1, keepdims=True).astype(jnp.int32)
    rank2 = jnp.sum(jnp.where(lane == i2, prefix, 0.0), axis=-1, keepdims=True).astype(jnp.int32)
    carry[...] = carry[...] + jnp.sum(hot.astype(F32), axis=0, keepdims=True)
    cnt_ref[...] = carry[...]

    ids_ref[...] = jnp.where(lane == 0, i1 - EXPERT_LANE0,
                             jnp.where(lane == 1, i2 - EXPERT_LANE0,
                                       jnp.where(lane == 2, rank1, jnp.where(lane == 3, rank2, 0))))
    gw_ref[...] = jnp.where(lane == 0, w1, jnp.where(lane == 1, w2, 0.0))


def _route(logits):
    t = logits.shape[0]
    tr = min(ROUTE_TILE, t)
    tile = pl.BlockSpec((tr, ROUTER_LANES), lambda i: (i, 0))
    return pl.pallas_call(
        _route_kernel,
        out_shape=(jax.ShapeDtypeStruct((t, ROUTER_LANES), jnp.int32),
                   jax.ShapeDtypeStruct((t, ROUTER_LANES), F32),
                   jax.ShapeDtypeStruct((1, ROUTER_LANES), F32)),
        grid=(t // tr,),
        in_specs=[tile],
        out_specs=(tile, tile, pl.BlockSpec((1, ROUTER_LANES), lambda i: (0, 0))),
        scratch_shapes=[pltpu.VMEM((1, ROUTER_LANES), F32)],
        compiler_params=_cparams("arbitrary"),
        name="route",
    )(logits)


def _row_copy(src_ref, src_row, dst_ref, dst_row, sem):
    rows = lambda tok: pl.ds(pl.multiple_of(tok * TOKEN_ROWS, TOKEN_ROWS), TOKEN_ROWS)
    return pltpu.make_async_copy(src_ref.at[rows(src_row)], dst_ref.at[rows(dst_row)], sem)


def _dispatch_kernel(dest_ref, pad_ref, h2_ref, xbuf_hbm, zero_s, sem):
    n_tok = h2_ref.shape[0] // TOKEN_ROWS
    n_pad = pad_ref.shape[0]
    zero_s[...] = jnp.zeros_like(zero_s)

    def start(i, carry):
        for k in range(TOP_K):
            _row_copy(h2_ref, i, xbuf_hbm, dest_ref[TOP_K * i + k], sem).start(priority=k)
        return carry

    def start_pad(i, carry):
        _row_copy(zero_s, 0, xbuf_hbm, pad_ref[i], sem).start()
        return carry

    def wait(i, carry):
        _row_copy(h2_ref, 0, xbuf_hbm, 0, sem).wait()
        return carry

    lax.fori_loop(0, n_tok, start, 0, unroll=DMA_UNROLL)
    lax.fori_loop(0, n_pad, start_pad, 0, unroll=DMA_UNROLL)
    lax.fori_loop(0, TOP_K * n_tok + n_pad, wait, 0, unroll=DMA_UNROLL)


def _dispatch(dest, pad_dest, h2, n_rows):
    t = h2.shape[0] // TOKEN_ROWS
    td = min(DMA_TILE, t)
    steps = t // td
    assert pad_dest.shape[0] % steps == 0 and t * TOP_K + pad_dest.shape[0] == n_rows
    return pl.pallas_call(
        _dispatch_kernel,
        out_shape=jax.ShapeDtypeStruct((n_rows * TOKEN_ROWS, LANES), jnp.uint32),
        grid=(steps,),
        in_specs=[pl.BlockSpec((TOP_K * td,), lambda i: (i,), memory_space=pltpu.SMEM),
                  pl.BlockSpec((pad_dest.shape[0] // steps,), lambda i: (i,), memory_space=pltpu.SMEM),
                  pl.BlockSpec((td * TOKEN_ROWS, LANES), lambda i: (i, 0))],
        out_specs=pl.BlockSpec(memory_space=pl.ANY),
        scratch_shapes=[pltpu.VMEM((TOKEN_ROWS, LANES), jnp.uint32), pltpu.SemaphoreType.DMA],
        compiler_params=_cparams("arbitrary"),
        name="dispatch",
    )(dest, pad_dest, h2)


def _expert_kernel(be_ref, used_ref, x_ref, wg_ref, wu_ref, wd_ref, y_ref, wg_s, wu_s, wd_s):
    i = pl.program_id(0)
    new_expert = (i == 0) | (be_ref[i] != be_ref[jnp.maximum(i - 1, 0)])
    in_use = i < used_ref[0]

    @pl.when(new_expert & in_use)
    def _():
        wg_s[...] = wg_ref[0, 0].astype(BF16)
        wu_s[...] = wu_ref[0, 0].astype(BF16)
        wd_s[...] = wd_ref[0, 0].astype(BF16)

    @pl.when(in_use)
    def _():
        x = _load_token_tiles(x_ref, MOE_BLOCK).astype(BF16)
        g = jnp.dot(x, wg_s[...], preferred_element_type=F32)
        u = jnp.dot(x, wu_s[...], preferred_element_type=F32)
        hid = (jax.nn.silu(g) * u).astype(BF16)
        _store_token_tiles(y_ref, jnp.dot(hid, wd_s[...], preferred_element_type=F32))

    @pl.when(jnp.logical_not(in_use))
    def _():
        y_ref[...] = jnp.zeros_like(y_ref)


def _experts(blk_expert, blocks_used, xbuf, wg, wu, wd, layer):
    d, de = wg.shape[2], wg.shape[3]
    n_blk = xbuf.shape[0] // (MOE_BLOCK * TOKEN_ROWS)
    rows = pl.BlockSpec((MOE_BLOCK * TOKEN_ROWS, LANES), lambda i, be, used: (i, 0))
    w_in = pl.BlockSpec((1, 1, d, de), lambda i, be, used: (layer, be[i], 0, 0))
    w_out = pl.BlockSpec((1, 1, de, d), lambda i, be, used: (layer, be[i], 0, 0))
    return pl.pallas_call(
        _expert_kernel,
        out_shape=jax.ShapeDtypeStruct(xbuf.shape, xbuf.dtype),
        grid_spec=pltpu.PrefetchScalarGridSpec(
            num_scalar_prefetch=2,
            grid=(n_blk,),
            in_specs=[rows, w_in, w_in, w_out],
            out_specs=rows,
            scratch_shapes=[pltpu.VMEM((d, de), BF16), pltpu.VMEM((d, de), BF16), pltpu.VMEM((de, d), BF16)]),
        compiler_params=_cparams("arbitrary"),
        name="experts",
    )(blk_expert, blocks_used, xbuf, wg, wu, wd)


def _combine_kernel(dest_ref, gw_ref, x_ref, ybuf_hbm, o_ref, rows0, rows1, sem):
    n_tok = x_ref.shape[0]
    rows = (rows0, rows1)

    def start(i, carry):
        for k in range(TOP_K):
            _row_copy(ybuf_hbm, dest_ref[TOP_K * i + k], rows[k], i, sem).start(priority=k)
        return carry

    def wait(i, carry):
        for k in range(TOP_K):
            _row_copy(ybuf_hbm, 0, rows[k], 0, sem).wait()
        return carry

    lax.fori_loop(0, n_tok, start, 0, unroll=DMA_UNROLL)
    lax.fori_loop(0, n_tok, wait, 0, unroll=DMA_UNROLL)
    gw = gw_ref[...]
    o_ref[...] = x_ref[...] + (gw[:, 0:1] * _load_token_tiles(rows0, n_tok)
                               + gw[:, 1:2] * _load_token_tiles(rows1, n_tok))


def _combine(dest, gw, x, ybuf):
    t, d = x.shape
    tc = min(DMA_TILE, t)
    tok = pl.BlockSpec((tc, d), lambda i: (i, 0))
    return pl.pallas_call(
        _combine_kernel,
        out_shape=jax.ShapeDtypeStruct((t, d), F32),
        grid=(t // tc,),
        in_specs=[pl.BlockSpec((TOP_K * tc,), lambda i: (i,), memory_space=pltpu.SMEM),
                  pl.BlockSpec((tc, ROUTER_LANES), lambda i: (i, 0)), tok,
                  pl.BlockSpec(memory_space=pl.ANY)],
        out_specs=tok,
        scratch_shapes=[pltpu.VMEM((tc * TOKEN_ROWS, LANES), ybuf.dtype), pltpu.VMEM((tc * TOKEN_ROWS, LANES), ybuf.dtype),
                        pltpu.SemaphoreType.DMA],
        compiler_params=_cparams("arbitrary"),
        name="combine",
    )(dest, gw, x, ybuf)


def _block_diag(w):
    nb, bd, _ = w.shape
    eye = jnp.eye(nb, dtype=w.dtype)
    return (eye[:, None, :, None] * w[:, :, None, :]).reshape(nb * bd, nb * bd)


def _moe(x, h2, logits, wg, wu, wd, layer):
    t, d = x.shape
    ids, gw, cnt = _route(logits)
    counts = cnt[0, EXPERT_LANE0:EXPERT_LANE0 + N_EXPERTS].astype(jnp.int32)
    padded = (counts + MOE_BLOCK - 1) // MOE_BLOCK * MOE_BLOCK
    pend = jnp.cumsum(padded)
    pstart = pend - padded
    lookup = lambda table, idx: jnp.sum(
        jnp.where(idx[..., None] == jnp.arange(table.shape[0], dtype=jnp.int32), table, 0), axis=-1)
    dest = (lookup(pstart, ids[:, 0:TOP_K]) + ids[:, TOP_K:2 * TOP_K]).reshape(t * TOP_K)
    n_rows = t * TOP_K + N_EXPERTS * MOE_BLOCK
    n_blk = n_rows // MOE_BLOCK
    blk_start = jnp.arange(n_blk, dtype=jnp.int32) * MOE_BLOCK
    blk_expert = jnp.minimum(jnp.sum(pend[None, :] <= blk_start[:, None], axis=1), N_EXPERTS - 1).astype(jnp.int32)
    n_pad = N_EXPERTS * MOE_BLOCK
    pad_cum = jnp.cumsum(padded - counts)
    seg_first = jnp.concatenate([pstart + counts, pend[-1:]])
    seg_skip = jnp.concatenate([jnp.zeros((1,), jnp.int32), pad_cum])
    j = jnp.arange(n_pad, dtype=jnp.int32)
    seg = jnp.sum(pad_cum[None, :] <= j[:, None], axis=1)
    pad_dest = (lookup(seg_first, seg) + (j - lookup(seg_skip, seg))).astype(jnp.int32)
    blocks_used = (pend[-1:] // MOE_BLOCK).astype(jnp.int32)
    xbuf = _dispatch(dest, pad_dest, h2, n_rows)
    ybuf = _experts(blk_expert, blocks_used, xbuf, wg, wu, wd, layer)
    return _combine(dest, gw, x, ybuf)


def kernel(x, norm_mix, w_in, conv_w, conv_b, lru_w_a, lru_b_a, lru_w_x, lru_b_x, lru_lambda, q_norm, k_norm, norm_out_lru, norm_out_attn, w_out, norm_ffn, router_group_w, router_group_b, router_expert_w, router_expert_b, w_gate, w_up, w_down):
    b, s, d = x.shape
    depth = w_in.shape[0]
    d_lru = conv_w.shape[2]
    d_attn = norm_out_attn.shape[1]
    pad = ROUTER_LANES - N_GROUPS - N_EXPERTS
    row = lambda v: v.reshape(1, -1)
    for l in range(depth):
        xl, gate, q, k, v = _in_proj(x, row(norm_mix[l]), w_in[l].astype(BF16), d_lru, d_attn)
        y_lru = _lru(xl, gate, conv_w[l], row(conv_b[l]),
                     _block_diag(lru_w_a[l]).astype(BF16), row(lru_b_a[l]),
                     _block_diag(lru_w_x[l]).astype(BF16), row(lru_b_x[l]), row(lru_lambda[l]))
        y_attn = _attention(q, k, v, row(jnp.tile(q_norm[l], HEADS_PER_SLAB)),
                            row(jnp.tile(k_norm[l], HEADS_PER_SLAB)))
        wr = jnp.pad(jnp.concatenate([router_group_w[l], router_expert_w[l]], axis=1), ((0, 0), (0, pad)))
        wr_hi = wr.astype(BF16)
        wr = jnp.concatenate([wr_hi, (wr - wr_hi.astype(F32)).astype(BF16)], axis=1)
        br = jnp.pad(jnp.concatenate([router_group_b[l], router_expert_b[l]]), (0, pad))
        xn, h2, logits = _out_proj(x, y_lru, y_attn, row(norm_out_lru[l]), row(norm_out_attn[l]),
                                   w_out[l].astype(BF16), row(norm_ffn[l]), wr, row(br))
        t = b * s
        x = _moe(xn.reshape(t, d), h2, logits.reshape(t, ROUTER_LANES),
                 w_gate, w_up, w_down, l).reshape(b, s, d)
    return x
```

```python
import functools

import jax
import jax.numpy as jnp
from jax import lax
from jax.experimental import pallas as pl
from jax.experimental.pallas import tpu as pltpu

F32 = jnp.float32
BF16 = jnp.bfloat16

EPS = 1e-6
LANES = 128
HEAD_DIM = 64
HEADS_PER_SLAB = LANES // HEAD_DIM
ATTN_BLOCK = 128
DILATIONS = (1, 4, 16)
CONV_WIDTH = 4
LRU_C = 8.0
N_GROUPS = 4
EXPERTS_PER_GROUP = 8
N_EXPERTS = N_GROUPS * EXPERTS_PER_GROUP
TOP_K = 2
ROUTER_LANES = LANES
EXPERT_LANE0 = N_GROUPS
VMEM_LIMIT = 56 * 1024 * 1024

TM_PROJ = 512
LRU_CHUNK = 256
SCAN_GROUP = 8
PAD_GROUP = 16
PAD_PITCH = 24
ATTN_UNROLL = 8
NORM_CHUNK = 512
ROUTE_TILE = 512
MOE_BLOCK = 256
TOKEN_ROWS = 4
DMA_TILE = 1024
DISPATCH_TILE = 2048
DMA_UNROLL = 8


def _rms(x, g):
    return x * lax.rsqrt(jnp.mean(x * x, axis=-1, keepdims=True) + EPS) * g


def _log1p(z):
    w = 1.0 + z
    return jnp.where(w == 1.0, z, z * jnp.log(w) / jnp.where(w == 1.0, 1.0, w - 1.0))


def _softplus(x):
    return jnp.maximum(x, 0.0) + _log1p(jnp.exp(-jnp.abs(x)))


def _store_token_tiles(ref, val):
    n, d = val.shape
    assert d == 2 * TOKEN_ROWS * LANES
    for r in range(TOKEN_ROWS):
        hi = val[:, r * LANES:(r + 1) * LANES]
        lo = val[:, d // 2 + r * LANES:d // 2 + (r + 1) * LANES]
        hi = lax.bitcast_convert_type(hi.astype(BF16).astype(F32), jnp.uint32)
        lo = lax.bitcast_convert_type(lo.astype(BF16).astype(F32), jnp.uint32)
        ref[pl.ds(r, n, stride=TOKEN_ROWS), :] = hi | (lo >> 16)


def _load_token_tiles(ref, n):
    words = [ref[pl.ds(r, n, stride=TOKEN_ROWS), :] for r in range(TOKEN_ROWS)]
    hi = [lax.bitcast_convert_type(w & jnp.uint32(0xFFFF0000), F32) for w in words]
    lo = [lax.bitcast_convert_type(w << 16, F32) for w in words]
    return jnp.concatenate(hi + lo, axis=-1)


def _cparams(*sem):
    return pltpu.CompilerParams(dimension_semantics=sem, vmem_limit_bytes=VMEM_LIMIT)


def _in_proj_kernel(x_ref, g_ref, w_ref, xl_ref, gate_ref, q_ref, k_ref, v_ref, *, d_lru, d_attn):
    h = _rms(x_ref[0], g_ref[...]).astype(BF16)

    def cols(lo, n):
        return jnp.dot(h, w_ref[:, lo:lo + n], preferred_element_type=F32)

    xl_ref[0] = cols(0, d_lru)
    gate_ref[0] = cols(d_lru, d_lru)
    base = 2 * d_lru
    for ref in (q_ref, k_ref, v_ref):
        heads = cols(base, d_attn)
        for j in range(d_attn // LANES):
            ref[0, j] = heads[:, j * LANES:(j + 1) * LANES]
        base += d_attn


def _in_proj(x, g, w, d_lru, d_attn):
    b, s, d = x.shape
    tm = min(TM_PROJ, s)
    n_slab = d_attn // LANES
    slab = jax.ShapeDtypeStruct((b, n_slab, s, LANES), F32)
    row = jax.ShapeDtypeStruct((b, s, d_lru), F32)
    slab_spec = pl.BlockSpec((1, n_slab, tm, LANES), lambda i, j: (i, 0, j, 0))
    row_spec = pl.BlockSpec((1, tm, d_lru), lambda i, j: (i, j, 0))
    return pl.pallas_call(
        functools.partial(_in_proj_kernel, d_lru=d_lru, d_attn=d_attn),
        out_shape=(row, row, slab, slab, slab),
        grid=(b, s // tm),
        in_specs=[pl.BlockSpec((1, tm, d), lambda i, j: (i, j, 0)),
                  pl.BlockSpec((1, d), lambda i, j: (0, 0)),
                  pl.BlockSpec(w.shape, lambda i, j: (0, 0))],
        out_specs=(row_spec, row_spec, slab_spec, slab_spec, slab_spec),
        compiler_params=_cparams("parallel", "parallel"),
        name="in_proj",
    )(x, g, w)


def _lru_kernel(xl_ref, gate_ref, cw_ref, cb_ref, wa_ref, ba_ref, wx_ref, bx_ref, lam_ref,
                y_ref, tail_scr, h_scr):
    tc = xl_ref.shape[1]
    halo = 8

    @pl.when(pl.program_id(1) == 0)
    def _():
        tail_scr[...] = jnp.zeros_like(tail_scr)
        h_scr[...] = jnp.zeros_like(h_scr)

    x = xl_ref[0]
    tail = tail_scr[...]
    first = lax.broadcasted_iota(jnp.int32, (halo, 1), 0)
    xc = cb_ref[...]
    for j in range(CONV_WIDTH):
        back = CONV_WIDTH - 1 - j
        if back == 0:
            xs = x
        else:
            shifted = pltpu.roll(x, back, axis=0)
            head = jnp.where(first < back, pltpu.roll(tail, back, axis=0), shifted[0:halo])
            xs = jnp.concatenate([head, shifted[halo:]], axis=0)
        xc = xc + cw_ref[j:j + 1, :] * xs
    tail_scr[...] = x[tc - halo:tc, :]

    xb = xc.astype(BF16)
    r = jax.nn.sigmoid(jnp.dot(xb, wa_ref[...], preferred_element_type=F32) + ba_ref[...])
    i = jax.nn.sigmoid(jnp.dot(xb, wx_ref[...], preferred_element_type=F32) + bx_ref[...])
    log_a = -LRU_C * r * _softplus(-lam_ref[...])
    a = jnp.exp(log_a)
    u = jnp.sqrt(jnp.tanh(-log_a) * (1.0 + a * a)) * (i * xc)

    in_group = lax.broadcasted_iota(jnp.int32, (tc, 1), 0) & (SCAN_GROUP - 1)
    shift = 1
    while shift < SCAN_GROUP:
        keep = in_group >= shift
        a_prev = pltpu.roll(a, shift, axis=0)
        u_prev = pltpu.roll(u, shift, axis=0)
        u = jnp.where(keep, a * u_prev + u, u)
        a = jnp.where(keep, a * a_prev, a)
        shift *= 2
    h_last = h_scr[...]
    groups = []
    for g in range(tc // SCAN_GROUP):
        rows = slice(g * SCAN_GROUP, (g + 1) * SCAN_GROUP)
        h_g = a[rows] * h_last + u[rows]
        h_last = h_g[SCAN_GROUP - 1:SCAN_GROUP, :]
        groups.append(h_g)
    h_scr[...] = h_last
    y_ref[0] = jnp.concatenate(groups, axis=0) * jax.nn.gelu(gate_ref[0])


def _lru(xl, gate, cw, cb, wa, ba, wx, bx, lam):
    b, s, c = xl.shape
    tc = min(LRU_CHUNK, s)
    row_spec = pl.BlockSpec((1, tc, c), lambda i, j: (i, j, 0))
    vec = pl.BlockSpec((1, c), lambda i, j: (0, 0))
    mat = pl.BlockSpec((c, c), lambda i, j: (0, 0))
    return pl.pallas_call(
        _lru_kernel,
        out_shape=jax.ShapeDtypeStruct((b, s, c), F32),
        grid=(b, s // tc),
        in_specs=[row_spec, row_spec, pl.BlockSpec((CONV_WIDTH, c), lambda i, j: (0, 0)), vec,
                  mat, vec, mat, vec, vec],
        out_specs=row_spec,
        scratch_shapes=[pltpu.VMEM((8, c), F32), pltpu.VMEM((1, c), F32)],
        compiler_params=_cparams("parallel", "arbitrary"),
        name="rg_lru",
    )(xl, gate, cw, cb, wa, ba, wx, bx, lam)


def _attn_kernel(q_ref, k_ref, v_ref, qg_ref, kg_ref, o_ref,
                 qn, kn, m_s, l_s, acc_s, qp, kp, vp, m_p, l_p, acc_p):
    s_len = q_ref.shape[2]
    blk = ATTN_BLOCK
    lane = lax.broadcasted_iota(jnp.int32, (1, LANES), 1)
    lo = lane < HEAD_DIM
    n_chunk = s_len // NORM_CHUNK
    groups = NORM_CHUNK // PAD_GROUP

    def padded_base(c):
        return pl.multiple_of(c * (groups * PAD_PITCH), 8)

    def to_padded(dst, c, val):
        for g in range(groups):
            dst[pl.ds(padded_base(c) + g * PAD_PITCH, PAD_GROUP), :] = val[g * PAD_GROUP:(g + 1) * PAD_GROUP]

    def from_padded(src, c):
        return jnp.concatenate(
            [src[pl.ds(padded_base(c) + g * PAD_PITCH, PAD_GROUP), :] for g in range(groups)], axis=0)

    same_head = (lax.broadcasted_iota(jnp.int32, (LANES, LANES), 0) // HEAD_DIM
                 == lax.broadcasted_iota(jnp.int32, (LANES, LANES), 1) // HEAD_DIM)
    head_mean = jnp.where(same_head, 1.0 / HEAD_DIM, 0.0).astype(BF16)

    def norm_body(c, carry):
        rows = pl.ds(pl.multiple_of(c * NORM_CHUNK, NORM_CHUNK), NORM_CHUNK)
        for src, g_ref, dst, dst_p, scale in ((q_ref, qg_ref, qn, qp, HEAD_DIM ** -0.5),
                                              (k_ref, kg_ref, kn, kp, 1.0)):
            x = src[0, 0, rows, :]
            x2 = x * x
            x2_hi = x2.astype(BF16)
            x2_lo = (x2 - x2_hi.astype(F32)).astype(BF16)
            ms = (jnp.dot(x2_hi, head_mean, preferred_element_type=F32)
                  + jnp.dot(x2_lo, head_mean, preferred_element_type=F32))
            y = x * lax.rsqrt(ms + EPS) * g_ref[...] * scale
            dst[rows, :] = y
            to_padded(dst_p, c, y)
        to_padded(vp, c, v_ref[0, 0, rows, :])
        return carry

    lax.fori_loop(0, n_chunk, norm_body, 0)

    rows2 = lax.broadcasted_iota(jnp.int32, (HEADS_PER_SLAB * blk, blk), 0)
    qi = rows2 & (blk - 1)
    kj = lax.broadcasted_iota(jnp.int32, (HEADS_PER_SLAB * blk, blk), 1)
    band_prev = kj >= qi
    band_cur = kj <= qi
    nt = (((1,), (1,)), ((), ()))

    def branch(d, q_src, k_src, v_src, stats, merge):
        m_r, l_r, acc_r = stats
        log_d = d.bit_length() - 1
        padded = d == PAD_GROUP
        pitch = PAD_PITCH if padded else d
        span = blk * pitch

        def sl(start):
            return pl.ds(start, blk) if pitch == 1 else pl.ds(start, blk, stride=pitch)

        def body(it, carry):
            n = it >> log_d
            r = it & (d - 1)
            cur = sl(n * span + r)
            prev = sl(jnp.maximum(n - 1, 0) * span + r)
            qt = q_src(cur)
            qs = jnp.concatenate([jnp.where(lo, qt, 0.0), jnp.where(lo, 0.0, qt)], axis=0).astype(BF16)
            s_c = lax.dot_general(qs, k_src(cur).astype(BF16), nt, preferred_element_type=F32)
            s_p = lax.dot_general(qs, k_src(prev).astype(BF16), nt, preferred_element_type=F32)
            s_c = jnp.where(band_cur, s_c, -jnp.inf)
            s_p = jnp.where(band_prev & (n > 0), s_p, -jnp.inf)
            m = jnp.max(jnp.maximum(s_c, s_p), axis=-1, keepdims=True)
            p_c = jnp.exp(s_c - m)
            p_p = jnp.exp(s_p - m)
            l = jnp.sum(p_c + p_p, axis=-1, keepdims=True)
            acc = (jnp.dot(p_c.astype(BF16), v_src(cur).astype(BF16), preferred_element_type=F32)
                   + jnp.dot(p_p.astype(BF16), v_src(prev).astype(BF16), preferred_element_type=F32))
            acc_t = jnp.where(lo, acc[:blk], acc[blk:])
            m_t = jnp.where(lo, jnp.broadcast_to(m[:blk], (blk, LANES)), jnp.broadcast_to(m[blk:], (blk, LANES)))
            l_t = jnp.where(lo, jnp.broadcast_to(l[:blk], (blk, LANES)), jnp.broadcast_to(l[blk:], (blk, LANES)))
            if merge:
                m_o = m_r[cur, :]
                m_n = jnp.maximum(m_o, m_t)
                c_o = jnp.exp(m_o - m_n)
                c_t = jnp.exp(m_t - m_n)
                m_r[cur, :] = m_n
                l_r[cur, :] = l_r[cur, :] * c_o + l_t * c_t
                acc_r[cur, :] = acc_r[cur, :] * c_o + acc_t * c_t
            else:
                m_r[cur, :] = m_t
                l_r[cur, :] = l_t
                acc_r[cur, :] = acc_t
            return carry

        lax.fori_loop(0, s_len // blk, body, 0, unroll=ATTN_UNROLL)

    token_order = (lambda s: qn[s, :], lambda s: kn[s, :], lambda s: v_ref[0, 0, s, :], (m_s, l_s, acc_s))
    padded_order = (lambda s: qp[s, :], lambda s: kp[s, :], lambda s: vp[s, :], (m_p, l_p, acc_p))
    for idx, d in enumerate(DILATIONS):
        if d == PAD_GROUP:
            branch(d, *padded_order, merge=False)
        else:
            branch(d, *token_order, merge=idx > 0)

    def out_body(c, carry):
        rows = pl.ds(pl.multiple_of(c * NORM_CHUNK, NORM_CHUNK), NORM_CHUNK)
        m_a, m_b = m_s[rows, :], from_padded(m_p, c)
        m_n = jnp.maximum(m_a, m_b)
        c_a = jnp.exp(m_a - m_n)
        c_b = jnp.exp(m_b - m_n)
        o_ref[0, 0, rows, :] = ((acc_s[rows, :] * c_a + from_padded(acc_p, c) * c_b)
                                / (l_s[rows, :] * c_a + from_padded(l_p, c) * c_b))
        return carry

    lax.fori_loop(0, n_chunk, out_body, 0)


def _attention(q, k, v, qg, kg):
    b, n_slab, s, _ = q.shape
    assert s % (ATTN_BLOCK * max(DILATIONS)) == 0 and s % NORM_CHUNK == 0 and max(DILATIONS) == PAD_GROUP
    slab_spec = pl.BlockSpec((1, 1, s, LANES), lambda i, j: (i, j, 0, 0))
    vec = pl.BlockSpec((1, LANES), lambda i, j: (0, 0))
    return pl.pallas_call(
        _attn_kernel,
        out_shape=jax.ShapeDtypeStruct(q.shape, F32),
        grid=(b, n_slab),
        in_specs=[slab_spec, slab_spec, slab_spec, vec, vec],
        out_specs=slab_spec,
        scratch_shapes=([pltpu.VMEM((s, LANES), F32) for _ in range(5)]
                        + [pltpu.VMEM((s // PAD_GROUP * PAD_PITCH, LANES), F32) for _ in range(6)]),
        compiler_params=_cparams("parallel", "parallel"),
        name="dilated_attn",
    )(q, k, v, qg, kg)


def _out_proj_kernel(x_ref, yl_ref, ya_ref, gl_ref, ga_ref, w_ref, gf_ref, wr_ref, br_ref,
                     xn_ref, h2_ref, lg_ref):
    d_lru = yl_ref.shape[2]
    nl = _rms(yl_ref[0], gl_ref[...]).astype(BF16)
    ya = jnp.concatenate([ya_ref[0, j] for j in range(ya_ref.shape[1])], axis=-1)
    na = _rms(ya, ga_ref[...]).astype(BF16)
    xn = x_ref[0] + (jnp.dot(nl, w_ref[0:d_lru, :], preferred_element_type=F32)
                     + jnp.dot(na, w_ref[d_lru:, :], preferred_element_type=F32))
    xn_ref[0] = xn
    h2 = _rms(xn, gf_ref[...])
    _store_token_tiles(h2_ref, h2)
    hi = h2.astype(BF16)
    lo = (h2 - hi.astype(F32)).astype(BF16)
    both = jnp.dot(hi, wr_ref[...], preferred_element_type=F32)
    lg_ref[0] = (both[:, :ROUTER_LANES] + both[:, ROUTER_LANES:]
                 + jnp.dot(lo, wr_ref[:, :ROUTER_LANES], preferred_element_type=F32) + br_ref[...])


def _out_proj(x, yl, ya, gl, ga, w, gf, wr, br):
    b, s, d = x.shape
    d_lru = yl.shape[2]
    n_slab = ya.shape[1]
    tm = min(TM_PROJ, s)
    nj = s // tm
    tok = pl.BlockSpec((1, tm, d), lambda i, j: (i, j, 0))
    tiles = pl.BlockSpec((tm * TOKEN_ROWS, LANES), lambda i, j: (i * nj + j, 0))
    const = lambda shape: pl.BlockSpec(shape, lambda i, j: (0,) * len(shape))
    return pl.pallas_call(
        _out_proj_kernel,
        out_shape=(jax.ShapeDtypeStruct((b, s, d), F32),
                   jax.ShapeDtypeStruct((b * s * TOKEN_ROWS, LANES), jnp.uint32),
                   jax.ShapeDtypeStruct((b, s, ROUTER_LANES), F32)),
        grid=(b, s // tm),
        in_specs=[tok, pl.BlockSpec((1, tm, d_lru), lambda i, j: (i, j, 0)),
                  pl.BlockSpec((1, n_slab, tm, LANES), lambda i, j: (i, 0, j, 0)),
                  const((1, d_lru)), const((1, n_slab * LANES)), const(w.shape), const((1, d)),
                  const(wr.shape), const((1, ROUTER_LANES))],
        out_specs=(tok, tiles, pl.BlockSpec((1, tm, ROUTER_LANES), lambda i, j: (i, j, 0))),
        compiler_params=_cparams("parallel", "parallel"),
        name="out_proj",
    )(x, yl, ya, gl, ga, w, gf, wr, br)


def _route_kernel(lg_ref, ids_ref, gw_ref, cnt_ref, carry):
    tr = lg_ref.shape[0]

    @pl.when(pl.program_id(0) == 0)
    def _():
        carry[...] = jnp.zeros_like(carry)

    lg = lg_ref[...]
    lane = lax.broadcasted_iota(jnp.int32, (tr, ROUTER_LANES), 1)
    big = jnp.int32(ROUTER_LANES)

    def argmax(vals):
        top = jnp.max(vals, axis=-1, keepdims=True)
        return top, jnp.min(jnp.where(vals == top, lane, big), axis=-1, keepdims=True)

    g_logit = jnp.where(lane < N_GROUPS, lg, -jnp.inf)
    g_top, g_idx = argmax(g_logit)
    p_top = 1.0 / jnp.sum(jnp.exp(g_logit - g_top), axis=-1, keepdims=True)
    e_lo = EXPERT_LANE0 + g_idx * EXPERTS_PER_GROUP
    e_logit = jnp.where((lane >= e_lo) & (lane < e_lo + EXPERTS_PER_GROUP), lg, -jnp.inf)
    v1, i1 = argmax(e_logit)
    v2, i2 = argmax(jnp.where(lane == i1, -jnp.inf, e_logit))
    e21 = jnp.exp(v2 - v1)
    w1 = 1.0 / (1.0 + e21) * p_top
    w2 = e21 / (1.0 + e21) * p_top

    hot = ((lane == i1) | (lane == i2))
    rr = lax.broadcasted_iota(jnp.int32, (tr, tr), 0)
    cc = lax.broadcasted_iota(jnp.int32, (tr, tr), 1)
    before = (rr > cc).astype(BF16)
    prefix = jnp.dot(before, hot.astype(BF16), preferred_element_type=F32) + carry[...]
    rank1 = jnp.sum(jnp.where(lane == i1, prefix, 0.0), axis=-1, keepdims=True).astype(jnp.int32)
    rank2 = jnp.sum(jnp.where(lane == i2, prefix, 0.0), axis=-1, keepdims=True).astype(jnp.int32)
    carry[...] = carry[...] + jnp.sum(hot.astype(F32), axis=0, keepdims=True)
    cnt_ref[...] = carry[...]

    ids_ref[...] = jnp.where(lane == 0, i1 - EXPERT_LANE0,
                             jnp.where(lane == 1, i2 - EXPERT_LANE0,
                                       jnp.where(lane == 2, rank1, jnp.where(lane == 3, rank2, 0))))
    gw_ref[...] = jnp.where(lane == 0, w1, jnp.where(lane == 1, w2, 0.0))


def _route(logits):
    t = logits.shape[0]
    tr = min(ROUTE_TILE, t)
    tile = pl.BlockSpec((tr, ROUTER_LANES), lambda i: (i, 0))
    return pl.pallas_call(
        _route_kernel,
        out_shape=(jax.ShapeDtypeStruct((t, ROUTER_LANES), jnp.int32),
                   jax.ShapeDtypeStruct((t, ROUTER_LANES), F32),
                   jax.ShapeDtypeStruct((1, ROUTER_LANES), F32)),
        grid=(t // tr,),
        in_specs=[tile],
        out_specs=(tile, tile, pl.BlockSpec((1, ROUTER_LANES), lambda i: (0, 0))),
        scratch_shapes=[pltpu.VMEM((1, ROUTER_LANES), F32)],
        compiler_params=_cparams("arbitrary"),
        name="route",
    )(logits)


def _row_copy(src_ref, src_row, dst_ref, dst_row, sem):
    rows = lambda tok: pl.ds(pl.multiple_of(tok * TOKEN_ROWS, TOKEN_ROWS), TOKEN_ROWS)
    return pltpu.make_async_copy(src_ref.at[rows(src_row)], dst_ref.at[rows(dst_row)], sem)


def _dispatch_kernel(dest_ref, pad_ref, h2_ref, xbuf_hbm, zero_s, sem):
    n_tok = h2_ref.shape[0] // TOKEN_ROWS
    n_pad = pad_ref.shape[0]
    zero_s[...] = jnp.zeros_like(zero_s)

    def start(i, carry):
        for k in range(TOP_K):
            _row_copy(h2_ref, i, xbuf_hbm, dest_ref[TOP_K * i + k], sem).start(priority=k)
        return carry

    def start_pad(i, carry):
        _row_copy(zero_s, 0, xbuf_hbm, pad_ref[i], sem).start()
        return carry

    def wait(i, carry):
        _row_copy(h2_ref, 0, xbuf_hbm, 0, sem).wait()
        return carry

    lax.fori_loop(0, n_tok, start, 0, unroll=DMA_UNROLL)
    lax.fori_loop(0, n_pad, start_pad, 0, unroll=DMA_UNROLL)
    lax.fori_loop(0, TOP_K * n_tok + n_pad, wait, 0, unroll=DMA_UNROLL)


def _dispatch(dest, pad_dest, h2, n_rows):
    t = h2.shape[0] // TOKEN_ROWS
    td = min(DISPATCH_TILE, t)
    steps = t // td
    assert pad_dest.shape[0] % steps == 0 and t * TOP_K + pad_dest.shape[0] == n_rows
    return pl.pallas_call(
        _dispatch_kernel,
        out_shape=jax.ShapeDtypeStruct((n_rows * TOKEN_ROWS, LANES), jnp.uint32),
        grid=(steps,),
        in_specs=[pl.BlockSpec((TOP_K * td,), lambda i: (i,), memory_space=pltpu.SMEM),
                  pl.BlockSpec((pad_dest.shape[0] // steps,), lambda i: (i,), memory_space=pltpu.SMEM),
                  pl.BlockSpec((td * TOKEN_ROWS, LANES), lambda i: (i, 0))],
        out_specs=pl.BlockSpec(memory_space=pl.ANY),
        scratch_shapes=[pltpu.VMEM((TOKEN_ROWS, LANES), jnp.uint32), pltpu.SemaphoreType.DMA],
        compiler_params=_cparams("arbitrary"),
        name="dispatch",
    )(dest, pad_dest, h2)


def _expert_kernel(be_ref, used_ref, x_ref, wg_ref, wu_ref, wd_ref, y_ref, wg_s, wu_s, wd_s):
    i = pl.program_id(0)
    new_expert = (i == 0) | (be_ref[i] != be_ref[jnp.maximum(i - 1, 0)])
    in_use = i < used_ref[0]

    @pl.when(new_expert & in_use)
    def _():
        wg_s[...] = wg_ref[0, 0].astype(BF16)
        wu_s[...] = wu_ref[0, 0].astype(BF16)
        wd_s[...] = wd_ref[0, 0].astype(BF16)

    @pl.when(in_use)
    def _():
        x = _load_token_tiles(x_ref, MOE_BLOCK).astype(BF16)
        g = jnp.dot(x, wg_s[...], preferred_element_type=F32)
        u = jnp.dot(x, wu_s[...], preferred_element_type=F32)
        hid = (jax.nn.silu(g) * u).astype(BF16)
        _store_token_tiles(y_ref, jnp.dot(hid, wd_s[...], preferred_element_type=F32))

    @pl.when(jnp.logical_not(in_use))
    def _():
        y_ref[...] = jnp.zeros_like(y_ref)


def _experts(blk_expert, blocks_used, xbuf, wg, wu, wd, layer):
    d, de = wg.shape[2], wg.shape[3]
    n_blk = xbuf.shape[0] // (MOE_BLOCK * TOKEN_ROWS)
    rows = pl.BlockSpec((MOE_BLOCK * TOKEN_ROWS, LANES), lambda i, be, used: (i, 0))
    w_in = pl.BlockSpec((1, 1, d, de), lambda i, be, used: (layer, be[i], 0, 0))
    w_out = pl.BlockSpec((1, 1, de, d), lambda i, be, used: (layer, be[i], 0, 0))
    return pl.pallas_call(
        _expert_kernel,
        out_shape=jax.ShapeDtypeStruct(xbuf.shape, xbuf.dtype),
        grid_spec=pltpu.PrefetchScalarGridSpec(
            num_scalar_prefetch=2,
            grid=(n_blk,),
            in_specs=[rows, w_in, w_in, w_out],
            out_specs=rows,
            scratch_shapes=[pltpu.VMEM((d, de), BF16), pltpu.VMEM((d, de), BF16), pltpu.VMEM((de, d), BF16)]),
        compiler_params=_cparams("arbitrary"),
        name="experts",
    )(blk_expert, blocks_used, xbuf, wg, wu, wd)


def _combine_kernel(dest_ref, dest_next_ref, gw_ref, x_ref, ybuf_hbm, o_ref, rows0, rows1, sems):
    n_tok = x_ref.shape[0]
    step = pl.program_id(0)
    slot = step & 1

    def issue(idx_ref, into):
        def start(i, carry):
            for k, rows in enumerate((rows0, rows1)):
                _row_copy(ybuf_hbm, idx_ref[TOP_K * i + k], rows.at[into], i, sems.at[into]).start(priority=k)
            return carry
        lax.fori_loop(0, n_tok, start, 0, unroll=DMA_UNROLL)

    @pl.when(step == 0)
    def _():
        issue(dest_ref, 0)

    @pl.when(step + 1 < pl.num_programs(0))
    def _():
        issue(dest_next_ref, 1 - slot)

    def wait(i, carry):
        for rows in (rows0, rows1):
            _row_copy(ybuf_hbm, 0, rows.at[slot], 0, sems.at[slot]).wait()
        return carry

    lax.fori_loop(0, n_tok, wait, 0, unroll=DMA_UNROLL)
    gw = gw_ref[...]
    o_ref[...] = x_ref[...] + (gw[:, 0:1] * _load_token_tiles(rows0.at[slot], n_tok)
                               + gw[:, 1:2] * _load_token_tiles(rows1.at[slot], n_tok))


def _combine(dest, gw, x, ybuf):
    t, d = x.shape
    tc = min(DMA_TILE, t)
    steps = t // tc
    tok = pl.BlockSpec((tc, d), lambda i: (i, 0))
    slots = pltpu.VMEM((2, tc * TOKEN_ROWS, LANES), ybuf.dtype)
    return pl.pallas_call(
        _combine_kernel,
        out_shape=jax.ShapeDtypeStruct((t, d), F32),
        grid=(steps,),
        in_specs=[pl.BlockSpec((TOP_K * tc,), lambda i: (i,), memory_space=pltpu.SMEM),
                  pl.BlockSpec((TOP_K * tc,), lambda i: (jnp.minimum(i + 1, steps - 1),), memory_space=pltpu.SMEM),
                  pl.BlockSpec((tc, ROUTER_LANES), lambda i: (i, 0)), tok,
                  pl.BlockSpec(memory_space=pl.ANY)],
        out_specs=tok,
        scratch_shapes=[slots, slots, pltpu.SemaphoreType.DMA((2,))],
        compiler_params=_cparams("arbitrary"),
        name="combine",
    )(dest, dest, gw, x, ybuf)


def _block_diag(w):
    nb, bd, _ = w.shape
    eye = jnp.eye(nb, dtype=w.dtype)
    return (eye[:, None, :, None] * w[:, :, None, :]).reshape(nb * bd, nb * bd)


def _moe(x, h2, logits, wg, wu, wd, layer):
    t, d = x.shape
    ids, gw, cnt = _route(logits)
    counts = cnt[0, EXPERT_LANE0:EXPERT_LANE0 + N_EXPERTS].astype(jnp.int32)
    padded = (counts + MOE_BLOCK - 1) // MOE_BLOCK * MOE_BLOCK
    pend = jnp.cumsum(padded)
    pstart = pend - padded
    lookup = lambda table, idx: jnp.sum(
        jnp.where(idx[..., None] == jnp.arange(table.shape[0], dtype=jnp.int32), table, 0), axis=-1)
    dest = (lookup(pstart, ids[:, 0:TOP_K]) + ids[:, TOP_K:2 * TOP_K]).reshape(t * TOP_K)
    n_rows = t * TOP_K + N_EXPERTS * MOE_BLOCK
    n_blk = n_rows // MOE_BLOCK
    blk_start = jnp.arange(n_blk, dtype=jnp.int32) * MOE_BLOCK
    blk_expert = jnp.minimum(jnp.sum(pend[None, :] <= blk_start[:, None], axis=1), N_EXPERTS - 1).astype(jnp.int32)
    n_pad = N_EXPERTS * MOE_BLOCK
    pad_cum = jnp.cumsum(padded - counts)
    seg_first = jnp.concatenate([pstart + counts, pend[-1:]])
    seg_skip = jnp.concatenate([jnp.zeros((1,), jnp.int32), pad_cum])
    j = jnp.arange(n_pad, dtype=jnp.int32)
    seg = jnp.sum(pad_cum[None, :] <= j[:, None], axis=1)
    pad_dest = (lookup(seg_first, seg) + (j - lookup(seg_skip, seg))).astype(jnp.int32)
    blocks_used = (pend[-1:] // MOE_BLOCK).astype(jnp.int32)
    xbuf = _dispatch(dest, pad_dest, h2, n_rows)
    ybuf = _experts(blk_expert, blocks_used, xbuf, wg, wu, wd, layer)
    return _combine(dest, gw, x, ybuf)


def kernel(x, norm_mix, w_in, conv_w, conv_b, lru_w_a, lru_b_a, lru_w_x, lru_b_x, lru_lambda, q_norm, k_norm, norm_out_lru, norm_out_attn, w_out, norm_ffn, router_group_w, router_group_b, router_expert_w, router_expert_b, w_gate, w_up, w_down):
    b, s, d = x.shape
    depth = w_in.shape[0]
    d_lru = conv_w.shape[2]
    d_attn = norm_out_attn.shape[1]
    pad = ROUTER_LANES - N_GROUPS - N_EXPERTS
    row = lambda v: v.reshape(1, -1)
    for l in range(depth):
        xl, gate, q, k, v = _in_proj(x, row(norm_mix[l]), w_in[l].astype(BF16), d_lru, d_attn)
        y_lru = _lru(xl, gate, conv_w[l], row(conv_b[l]),
                     _block_diag(lru_w_a[l]).astype(BF16), row(lru_b_a[l]),
                     _block_diag(lru_w_x[l]).astype(BF16), row(lru_b_x[l]), row(lru_lambda[l]))
        y_attn = _attention(q, k, v, row(jnp.tile(q_norm[l], HEADS_PER_SLAB)),
                            row(jnp.tile(k_norm[l], HEADS_PER_SLAB)))
        wr = jnp.pad(jnp.concatenate([router_group_w[l], router_expert_w[l]], axis=1), ((0, 0), (0, pad)))
        wr_hi = wr.astype(BF16)
        wr = jnp.concatenate([wr_hi, (wr - wr_hi.astype(F32)).astype(BF16)], axis=1)
        br = jnp.pad(jnp.concatenate([router_group_b[l], router_expert_b[l]]), (0, pad))
        xn, h2, logits = _out_proj(x, y_lru, y_attn, row(norm_out_lru[l]), row(norm_out_attn[l]),
                                   w_out[l].astype(BF16), row(norm_ffn[l]), wr, row(br))
        t = b * s
        x = _moe(xn.reshape(t, d), h2, logits.reshape(t, ROUTER_LANES),
                 w_gate, w_up, w_down, l).reshape(b, s, d)
    return x
```

```python
import functools

import jax
import jax.numpy as jnp
from jax import lax
from jax.experimental import pallas as pl
from jax.experimental.pallas import tpu as pltpu

F32 = jnp.float32
BF16 = jnp.bfloat16

EPS = 1e-6
LANES = 128
HEAD_DIM = 64
HEADS_PER_SLAB = LANES // HEAD_DIM
ATTN_BLOCK = 128
DILATIONS = (1, 4, 16)
CONV_WIDTH = 4
LRU_C = 8.0
N_GROUPS = 4
EXPERTS_PER_GROUP = 8
N_EXPERTS = N_GROUPS * EXPERTS_PER_GROUP
TOP_K = 2
ROUTER_LANES = LANES
EXPERT_LANE0 = N_GROUPS
VMEM_LIMIT = 56 * 1024 * 1024

TM_PROJ = 512
SCAN_GROUP = 8
PAD_GROUP = 16
PAD_PITCH = 24
ATTN_UNROLL = 8
NORM_CHUNK = 512
MOE_BLOCK = 512
TOKEN_ROWS = 4
DMA_TILE = 1024
DISPATCH_TILE = 1024
DMA_UNROLL = 8


def _rms(x, g):
    return x * lax.rsqrt(jnp.mean(x * x, axis=-1, keepdims=True) + EPS) * g


def _log1p(z):
    w = 1.0 + z
    return jnp.where(w == 1.0, z, z * jnp.log(w) / jnp.where(w == 1.0, 1.0, w - 1.0))


def _softplus(x):
    return jnp.maximum(x, 0.0) + _log1p(jnp.exp(-jnp.abs(x)))


def _store_token_tiles(ref, val):
    n, d = val.shape
    assert d == 2 * TOKEN_ROWS * LANES
    for r in range(TOKEN_ROWS):
        hi = val[:, r * LANES:(r + 1) * LANES]
        lo = val[:, d // 2 + r * LANES:d // 2 + (r + 1) * LANES]
        hi = lax.bitcast_convert_type(hi.astype(BF16).astype(F32), jnp.uint32)
        lo = lax.bitcast_convert_type(lo.astype(BF16).astype(F32), jnp.uint32)
        ref[pl.ds(r, n, stride=TOKEN_ROWS), :] = hi | (lo >> 16)


def _load_token_tiles(ref, n):
    words = [ref[pl.ds(r, n, stride=TOKEN_ROWS), :] for r in range(TOKEN_ROWS)]
    hi = [lax.bitcast_convert_type(w & jnp.uint32(0xFFFF0000), F32) for w in words]
    lo = [lax.bitcast_convert_type(w << 16, F32) for w in words]
    return jnp.concatenate(hi + lo, axis=-1)


def _cparams(*sem):
    return pltpu.CompilerParams(dimension_semantics=sem, vmem_limit_bytes=VMEM_LIMIT)


def _in_proj_kernel(x_ref, g_ref, w_ref, cw_ref, cb_ref, wa_ref, ba_ref, wx_ref, bx_ref, lam_ref,
                    y_ref, q_ref, k_ref, v_ref, tail_scr, h_scr, *, d_lru, d_attn):
    h = _rms(x_ref[0], g_ref[...]).astype(BF16)

    def cols(lo, n):
        return jnp.dot(h, w_ref[:, lo:lo + n], preferred_element_type=F32)

    @pl.when(pl.program_id(1) == 0)
    def _():
        tail_scr[...] = jnp.zeros_like(tail_scr)
        h_scr[...] = jnp.zeros_like(h_scr)

    y_ref[0] = _rg_lru_tile(cols(0, d_lru), cols(d_lru, d_lru), cw_ref, cb_ref, wa_ref, ba_ref,
                            wx_ref, bx_ref, lam_ref, tail_scr, h_scr)
    base = 2 * d_lru
    for ref in (q_ref, k_ref, v_ref):
        heads = cols(base, d_attn)
        for j in range(d_attn // LANES):
            ref[0, j] = heads[:, j * LANES:(j + 1) * LANES]
        base += d_attn


def _in_proj(x, g, w, cw, cb, wa, ba, wx, bx, lam, d_attn):
    b, s, d = x.shape
    d_lru = cw.shape[1]
    tm = min(TM_PROJ, s)
    n_slab = d_attn // LANES
    slab = jax.ShapeDtypeStruct((b, n_slab, s, LANES), F32)
    slab_spec = pl.BlockSpec((1, n_slab, tm, LANES), lambda i, j: (i, 0, j, 0))
    const = lambda shape: pl.BlockSpec(shape, lambda i, j: (0,) * len(shape))
    vec = const((1, d_lru))
    return pl.pallas_call(
        functools.partial(_in_proj_kernel, d_lru=d_lru, d_attn=d_attn),
        out_shape=(jax.ShapeDtypeStruct((b, s, d_lru), F32), slab, slab, slab),
        grid=(b, s // tm),
        in_specs=[pl.BlockSpec((1, tm, d), lambda i, j: (i, j, 0)), const((1, d)), const(w.shape),
                  const((CONV_WIDTH, d_lru)), vec, const(wa.shape), vec, const(wx.shape), vec, vec],
        out_specs=(pl.BlockSpec((1, tm, d_lru), lambda i, j: (i, j, 0)), slab_spec, slab_spec, slab_spec),
        scratch_shapes=[pltpu.VMEM((8, d_lru), F32), pltpu.VMEM((1, d_lru), F32)],
        compiler_params=_cparams("parallel", "arbitrary"),
        name="in_proj",
    )(x, g, w, cw, cb, wa, ba, wx, bx, lam)


def _rg_lru_tile(x, gate, cw_ref, cb_ref, wa_ref, ba_ref, wx_ref, bx_ref, lam_ref, tail_scr, h_scr):
    tc = x.shape[0]
    halo = tail_scr.shape[0]
    tail = tail_scr[...]
    first = lax.broadcasted_iota(jnp.int32, (halo, 1), 0)
    xc = cb_ref[...]
    for j in range(CONV_WIDTH):
        back = CONV_WIDTH - 1 - j
        if back == 0:
            xs = x
        else:
            shifted = pltpu.roll(x, back, axis=0)
            head = jnp.where(first < back, pltpu.roll(tail, back, axis=0), shifted[0:halo])
            xs = jnp.concatenate([head, shifted[halo:]], axis=0)
        xc = xc + cw_ref[j:j + 1, :] * xs
    tail_scr[...] = x[tc - halo:tc, :]

    xb = xc.astype(BF16)
    r = jax.nn.sigmoid(jnp.dot(xb, wa_ref[...], preferred_element_type=F32) + ba_ref[...])
    i = jax.nn.sigmoid(jnp.dot(xb, wx_ref[...], preferred_element_type=F32) + bx_ref[...])
    log_a = -LRU_C * r * _softplus(-lam_ref[...])
    a = jnp.exp(log_a)
    u = jnp.sqrt(jnp.tanh(-log_a) * (1.0 + a * a)) * (i * xc)

    in_group = lax.broadcasted_iota(jnp.int32, (tc, 1), 0) & (SCAN_GROUP - 1)
    shift = 1
    while shift < SCAN_GROUP:
        keep = in_group >= shift
        a_prev = pltpu.roll(a, shift, axis=0)
        u_prev = pltpu.roll(u, shift, axis=0)
        u = jnp.where(keep, a * u_prev + u, u)
        a = jnp.where(keep, a * a_prev, a)
        shift *= 2
    h_last = h_scr[...]
    groups = []
    for g in range(tc // SCAN_GROUP):
        rows = slice(g * SCAN_GROUP, (g + 1) * SCAN_GROUP)
        h_g = a[rows] * h_last + u[rows]
        h_last = h_g[SCAN_GROUP - 1:SCAN_GROUP, :]
        groups.append(h_g)
    h_scr[...] = h_last
    return jnp.concatenate(groups, axis=0) * jax.nn.gelu(gate)


def _attn_kernel(q_ref, k_ref, v_ref, qg_ref, kg_ref, o_ref,
                 qn, kn, m_s, l_s, acc_s, qp, kp, vp, m_p, l_p, acc_p):
    s_len = q_ref.shape[2]
    blk = ATTN_BLOCK
    lane = lax.broadcasted_iota(jnp.int32, (1, LANES), 1)
    lo = lane < HEAD_DIM
    n_chunk = s_len // NORM_CHUNK
    groups = NORM_CHUNK // PAD_GROUP

    def padded_base(c):
        return pl.multiple_of(c * (groups * PAD_PITCH), 8)

    def to_padded(dst, c, val):
        for g in range(groups):
            dst[pl.ds(padded_base(c) + g * PAD_PITCH, PAD_GROUP), :] = val[g * PAD_GROUP:(g + 1) * PAD_GROUP]

    def from_padded(src, c):
        return jnp.concatenate(
            [src[pl.ds(padded_base(c) + g * PAD_PITCH, PAD_GROUP), :] for g in range(groups)], axis=0)

    same_head = (lax.broadcasted_iota(jnp.int32, (LANES, LANES), 0) // HEAD_DIM
                 == lax.broadcasted_iota(jnp.int32, (LANES, LANES), 1) // HEAD_DIM)
    head_mean = jnp.where(same_head, 1.0 / HEAD_DIM, 0.0).astype(BF16)

    def norm_body(c, carry):
        rows = pl.ds(pl.multiple_of(c * NORM_CHUNK, NORM_CHUNK), NORM_CHUNK)
        for src, g_ref, dst, dst_p, scale in ((q_ref, qg_ref, qn, qp, HEAD_DIM ** -0.5),
                                              (k_ref, kg_ref, kn, kp, 1.0)):
            x = src[0, 0, rows, :]
            x2 = x * x
            x2_hi = x2.astype(BF16)
            x2_lo = (x2 - x2_hi.astype(F32)).astype(BF16)
            ms = (jnp.dot(x2_hi, head_mean, preferred_element_type=F32)
                  + jnp.dot(x2_lo, head_mean, preferred_element_type=F32))
            y = x * lax.rsqrt(ms + EPS) * g_ref[...] * scale
            dst[rows, :] = y
            to_padded(dst_p, c, y)
        to_padded(vp, c, v_ref[0, 0, rows, :])
        return carry

    lax.fori_loop(0, n_chunk, norm_body, 0)

    rows2 = lax.broadcasted_iota(jnp.int32, (HEADS_PER_SLAB * blk, blk), 0)
    qi = rows2 & (blk - 1)
    kj = lax.broadcasted_iota(jnp.int32, (HEADS_PER_SLAB * blk, blk), 1)
    band_prev = kj >= qi
    band_cur = kj <= qi
    nt = (((1,), (1,)), ((), ()))

    def branch(d, q_src, k_src, v_src, stats, merge):
        m_r, l_r, acc_r = stats
        log_d = d.bit_length() - 1
        padded = d == PAD_GROUP
        pitch = PAD_PITCH if padded else d
        span = blk * pitch

        def sl(start):
            return pl.ds(start, blk) if pitch == 1 else pl.ds(start, blk, stride=pitch)

        def body(it, carry):
            n = it >> log_d
            r = it & (d - 1)
            cur = sl(n * span + r)
            prev = sl(jnp.maximum(n - 1, 0) * span + r)
            qt = q_src(cur)
            qs = jnp.concatenate([jnp.where(lo, qt, 0.0), jnp.where(lo, 0.0, qt)], axis=0).astype(BF16)
            s_c = lax.dot_general(qs, k_src(cur).astype(BF16), nt, preferred_element_type=F32)
            s_p = lax.dot_general(qs, k_src(prev).astype(BF16), nt, preferred_element_type=F32)
            s_c = jnp.where(band_cur, s_c, -jnp.inf)
            s_p = jnp.where(band_prev & (n > 0), s_p, -jnp.inf)
            m = jnp.max(jnp.maximum(s_c, s_p), axis=-1, keepdims=True)
            p_c = jnp.exp(s_c - m)
            p_p = jnp.exp(s_p - m)
            l = jnp.sum(p_c + p_p, axis=-1, keepdims=True)
            acc = (jnp.dot(p_c.astype(BF16), v_src(cur).astype(BF16), preferred_element_type=F32)
                   + jnp.dot(p_p.astype(BF16), v_src(prev).astype(BF16), preferred_element_type=F32))
            acc_t = jnp.where(lo, acc[:blk], acc[blk:])
            m_t = jnp.where(lo, jnp.broadcast_to(m[:blk], (blk, LANES)), jnp.broadcast_to(m[blk:], (blk, LANES)))
            l_t = jnp.where(lo, jnp.broadcast_to(l[:blk], (blk, LANES)), jnp.broadcast_to(l[blk:], (blk, LANES)))
            if merge:
                m_o = m_r[cur, :]
                m_n = jnp.maximum(m_o, m_t)
                c_o = jnp.exp(m_o - m_n)
                c_t = jnp.exp(m_t - m_n)
                m_r[cur, :] = m_n
                l_r[cur, :] = l_r[cur, :] * c_o + l_t * c_t
                acc_r[cur, :] = acc_r[cur, :] * c_o + acc_t * c_t
            else:
                m_r[cur, :] = m_t
                l_r[cur, :] = l_t
                acc_r[cur, :] = acc_t
            return carry

        lax.fori_loop(0, s_len // blk, body, 0, unroll=ATTN_UNROLL)

    token_order = (lambda s: qn[s, :], lambda s: kn[s, :], lambda s: v_ref[0, 0, s, :], (m_s, l_s, acc_s))
    padded_order = (lambda s: qp[s, :], lambda s: kp[s, :], lambda s: vp[s, :], (m_p, l_p, acc_p))
    for idx, d in enumerate(DILATIONS):
        if d == PAD_GROUP:
            branch(d, *padded_order, merge=False)
        else:
            branch(d, *token_order, merge=idx > 0)

    def out_body(c, carry):
        rows = pl.ds(pl.multiple_of(c * NORM_CHUNK, NORM_CHUNK), NORM_CHUNK)
        m_a, m_b = m_s[rows, :], from_padded(m_p, c)
        m_n = jnp.maximum(m_a, m_b)
        c_a = jnp.exp(m_a - m_n)
        c_b = jnp.exp(m_b - m_n)
        o_ref[0, 0, rows, :] = ((acc_s[rows, :] * c_a + from_padded(acc_p, c) * c_b)
                                / (l_s[rows, :] * c_a + from_padded(l_p, c) * c_b))
        return carry

    lax.fori_loop(0, n_chunk, out_body, 0)


def _attention(q, k, v, qg, kg):
    b, n_slab, s, _ = q.shape
    assert s % (ATTN_BLOCK * max(DILATIONS)) == 0 and s % NORM_CHUNK == 0 and max(DILATIONS) == PAD_GROUP
    slab_spec = pl.BlockSpec((1, 1, s, LANES), lambda i, j: (i, j, 0, 0))
    vec = pl.BlockSpec((1, LANES), lambda i, j: (0, 0))
    return pl.pallas_call(
        _attn_kernel,
        out_shape=jax.ShapeDtypeStruct(q.shape, F32),
        grid=(b, n_slab),
        in_specs=[slab_spec, slab_spec, slab_spec, vec, vec],
        out_specs=slab_spec,
        scratch_shapes=([pltpu.VMEM((s, LANES), F32) for _ in range(5)]
                        + [pltpu.VMEM((s // PAD_GROUP * PAD_PITCH, LANES), F32) for _ in range(6)]),
        compiler_params=_cparams("parallel", "parallel"),
        name="dilated_attn",
    )(q, k, v, qg, kg)


def _out_proj_kernel(x_ref, yl_ref, ya_ref, gl_ref, ga_ref, w_ref, gf_ref, wr_ref, br_ref,
                     xn_ref, h2_ref, ids_ref, gw_ref, cnt_ref, carry):
    d_lru = yl_ref.shape[2]
    nl = _rms(yl_ref[0], gl_ref[...]).astype(BF16)
    ya = jnp.concatenate([ya_ref[0, j] for j in range(ya_ref.shape[1])], axis=-1)
    na = _rms(ya, ga_ref[...]).astype(BF16)
    xn = x_ref[0] + (jnp.dot(nl, w_ref[0:d_lru, :], preferred_element_type=F32)
                     + jnp.dot(na, w_ref[d_lru:, :], preferred_element_type=F32))
    xn_ref[0] = xn
    h2 = _rms(xn, gf_ref[...])
    _store_token_tiles(h2_ref, h2)
    hi = h2.astype(BF16)
    lo = (h2 - hi.astype(F32)).astype(BF16)
    both = jnp.dot(hi, wr_ref[...], preferred_element_type=F32)
    logits = (both[:, :ROUTER_LANES] + both[:, ROUTER_LANES:]
              + jnp.dot(lo, wr_ref[:, :ROUTER_LANES], preferred_element_type=F32) + br_ref[...])

    @pl.when((pl.program_id(0) == 0) & (pl.program_id(1) == 0))
    def _():
        carry[...] = jnp.zeros_like(carry)

    ids_ref[0], gw_ref[0] = _route_tile(logits, carry)
    cnt_ref[...] = carry[...]


def _out_proj(x, yl, ya, gl, ga, w, gf, wr, br):
    b, s, d = x.shape
    d_lru = yl.shape[2]
    n_slab = ya.shape[1]
    tm = min(TM_PROJ, s)
    nj = s // tm
    tok = pl.BlockSpec((1, tm, d), lambda i, j: (i, j, 0))
    tiles = pl.BlockSpec((tm * TOKEN_ROWS, LANES), lambda i, j: (i * nj + j, 0))
    lanes = pl.BlockSpec((1, tm, ROUTER_LANES), lambda i, j: (i, j, 0))
    const = lambda shape: pl.BlockSpec(shape, lambda i, j: (0,) * len(shape))
    return pl.pallas_call(
        _out_proj_kernel,
        out_shape=(jax.ShapeDtypeStruct((b, s, d), F32),
                   jax.ShapeDtypeStruct((b * s * TOKEN_ROWS, LANES), jnp.uint32),
                   jax.ShapeDtypeStruct((b, s, ROUTER_LANES), jnp.int32),
                   jax.ShapeDtypeStruct((b, s, ROUTER_LANES), F32),
                   jax.ShapeDtypeStruct((1, ROUTER_LANES), F32)),
        grid=(b, s // tm),
        in_specs=[tok, pl.BlockSpec((1, tm, d_lru), lambda i, j: (i, j, 0)),
                  pl.BlockSpec((1, n_slab, tm, LANES), lambda i, j: (i, 0, j, 0)),
                  const((1, d_lru)), const((1, n_slab * LANES)), const(w.shape), const((1, d)),
                  const(wr.shape), const((1, ROUTER_LANES))],
        out_specs=(tok, tiles, lanes, lanes, const((1, ROUTER_LANES))),
        scratch_shapes=[pltpu.VMEM((1, ROUTER_LANES), F32)],
        compiler_params=_cparams("arbitrary", "arbitrary"),
        name="out_proj",
    )(x, yl, ya, gl, ga, w, gf, wr, br)


def _route_tile(lg, carry):
    tr = lg.shape[0]
    lane = lax.broadcasted_iota(jnp.int32, (tr, ROUTER_LANES), 1)
    big = jnp.int32(ROUTER_LANES)

    def argmax(vals):
        top = jnp.max(vals, axis=-1, keepdims=True)
        return top, jnp.min(jnp.where(vals == top, lane, big), axis=-1, keepdims=True)

    g_logit = jnp.where(lane < N_GROUPS, lg, -jnp.inf)
    g_top, g_idx = argmax(g_logit)
    p_top = 1.0 / jnp.sum(jnp.exp(g_logit - g_top), axis=-1, keepdims=True)
    e_lo = EXPERT_LANE0 + g_idx * EXPERTS_PER_GROUP
    e_logit = jnp.where((lane >= e_lo) & (lane < e_lo + EXPERTS_PER_GROUP), lg, -jnp.inf)
    v1, i1 = argmax(e_logit)
    v2, i2 = argmax(jnp.where(lane == i1, -jnp.inf, e_logit))
    e21 = jnp.exp(v2 - v1)
    w1 = 1.0 / (1.0 + e21) * p_top
    w2 = e21 / (1.0 + e21) * p_top

    hot = ((lane == i1) | (lane == i2))
    rr = lax.broadcasted_iota(jnp.int32, (tr, tr), 0)
    cc = lax.broadcasted_iota(jnp.int32, (tr, tr), 1)
    before = (rr > cc).astype(BF16)
    prefix = jnp.dot(before, hot.astype(BF16), preferred_element_type=F32) + carry[...]
    rank1 = jnp.sum(jnp.where(lane == i1, prefix, 0.0), axis=-1, keepdims=True).astype(jnp.int32)
    rank2 = jnp.sum(jnp.where(lane == i2, prefix, 0.0), axis=-1, keepdims=True).astype(jnp.int32)
    carry[...] = carry[...] + jnp.sum(hot.astype(F32), axis=0, keepdims=True)

    ids = jnp.where(lane == 0, i1 - EXPERT_LANE0,
                    jnp.where(lane == 1, i2 - EXPERT_LANE0,
                              jnp.where(lane == 2, rank1, jnp.where(lane == 3, rank2, 0))))
    return ids, jnp.where(lane == 0, w1, jnp.where(lane == 1, w2, 0.0))


def _row_copy(src_ref, src_row, dst_ref, dst_row, sem):
    rows = lambda tok: pl.ds(pl.multiple_of(tok * TOKEN_ROWS, TOKEN_ROWS), TOKEN_ROWS)
    return pltpu.make_async_copy(src_ref.at[rows(src_row)], dst_ref.at[rows(dst_row)], sem)


def _dispatch_kernel(dest_ref, pad_ref, h2_ref, xbuf_hbm, zero_s, sem):
    n_tok = h2_ref.shape[0] // TOKEN_ROWS
    n_pad = pad_ref.shape[0]
    zero_s[...] = jnp.zeros_like(zero_s)

    def start(i, carry):
        for k in range(TOP_K):
            _row_copy(h2_ref, i, xbuf_hbm, dest_ref[TOP_K * i + k], sem).start(priority=k)
        return carry

    def start_pad(i, carry):
        _row_copy(zero_s, 0, xbuf_hbm, pad_ref[i], sem).start()
        return carry

    def wait(i, carry):
        _row_copy(h2_ref, 0, xbuf_hbm, 0, sem).wait()
        return carry

    lax.fori_loop(0, n_tok, start, 0, unroll=DMA_UNROLL)
    lax.fori_loop(0, n_pad, start_pad, 0, unroll=DMA_UNROLL)
    lax.fori_loop(0, TOP_K * n_tok + n_pad, wait, 0, unroll=DMA_UNROLL)


def _dispatch(dest, pad_dest, h2, n_rows):
    t = h2.shape[0] // TOKEN_ROWS
    td = min(DISPATCH_TILE, t)
    steps = t // td
    assert pad_dest.shape[0] % steps == 0 and t * TOP_K + pad_dest.shape[0] == n_rows
    return pl.pallas_call(
        _dispatch_kernel,
        out_shape=jax.ShapeDtypeStruct((n_rows * TOKEN_ROWS, LANES), jnp.uint32),
        grid=(steps,),
        in_specs=[pl.BlockSpec((TOP_K * td,), lambda i: (i,), memory_space=pltpu.SMEM),
                  pl.BlockSpec((pad_dest.shape[0] // steps,), lambda i: (i,), memory_space=pltpu.SMEM),
                  pl.BlockSpec((td * TOKEN_ROWS, LANES), lambda i: (i, 0))],
        out_specs=pl.BlockSpec(memory_space=pl.ANY),
        scratch_shapes=[pltpu.VMEM((TOKEN_ROWS, LANES), jnp.uint32), pltpu.SemaphoreType.DMA],
        compiler_params=_cparams("arbitrary"),
        name="dispatch",
    )(dest, pad_dest, h2)


def _expert_kernel(be_ref, used_ref, x_ref, wg_ref, wu_ref, wd_ref, y_ref, wg_s, wu_s, wd_s):
    i = pl.program_id(0)
    new_expert = (i == 0) | (be_ref[i] != be_ref[jnp.maximum(i - 1, 0)])
    in_use = i < used_ref[0]

    @pl.when(new_expert & in_use)
    def _():
        wg_s[...] = wg_ref[0, 0].astype(BF16)
        wu_s[...] = wu_ref[0, 0].astype(BF16)
        wd_s[...] = wd_ref[0, 0].astype(BF16)

    @pl.when(in_use)
    def _():
        x = _load_token_tiles(x_ref, MOE_BLOCK).astype(BF16)
        g = jnp.dot(x, wg_s[...], preferred_element_type=F32)
        u = jnp.dot(x, wu_s[...], preferred_element_type=F32)
        hid = (jax.nn.silu(g) * u).astype(BF16)
        _store_token_tiles(y_ref, jnp.dot(hid, wd_s[...], preferred_element_type=F32))

    @pl.when(jnp.logical_not(in_use))
    def _():
        y_ref[...] = jnp.zeros_like(y_ref)


def _experts(blk_expert, blocks_used, xbuf, wg, wu, wd, layer):
    d, de = wg.shape[2], wg.shape[3]
    n_blk = xbuf.shape[0] // (MOE_BLOCK * TOKEN_ROWS)
    rows = pl.BlockSpec((MOE_BLOCK * TOKEN_ROWS, LANES), lambda i, be, used: (i, 0))
    w_in = pl.BlockSpec((1, 1, d, de), lambda i, be, used: (layer, be[i], 0, 0))
    w_out = pl.BlockSpec((1, 1, de, d), lambda i, be, used: (layer, be[i], 0, 0))
    return pl.pallas_call(
        _expert_kernel,
        out_shape=jax.ShapeDtypeStruct(xbuf.shape, xbuf.dtype),
        grid_spec=pltpu.PrefetchScalarGridSpec(
            num_scalar_prefetch=2,
            grid=(n_blk,),
            in_specs=[rows, w_in, w_in, w_out],
            out_specs=rows,
            scratch_shapes=[pltpu.VMEM((d, de), BF16), pltpu.VMEM((d, de), BF16), pltpu.VMEM((de, d), BF16)]),
        compiler_params=_cparams("arbitrary"),
        name="experts",
    )(blk_expert, blocks_used, xbuf, wg, wu, wd)


def _combine_kernel(dest_ref, dest_next_ref, gw_ref, x_ref, ybuf_hbm, o_ref, rows0, rows1, sems):
    n_tok = x_ref.shape[0]
    step = pl.program_id(0)
    slot = step & 1

    def issue(idx_ref, into):
        def start(i, carry):
            for k, rows in enumerate((rows0, rows1)):
                _row_copy(ybuf_hbm, idx_ref[TOP_K * i + k], rows.at[into], i, sems.at[into]).start(priority=k)
            return carry
        lax.fori_loop(0, n_tok, start, 0, unroll=DMA_UNROLL)

    @pl.when(step == 0)
    def _():
        issue(dest_ref, 0)

    @pl.when(step + 1 < pl.num_programs(0))
    def _():
        issue(dest_next_ref, 1 - slot)

    def wait(i, carry):
        for rows in (rows0, rows1):
            _row_copy(ybuf_hbm, 0, rows.at[slot], 0, sems.at[slot]).wait()
        return carry

    lax.fori_loop(0, n_tok, wait, 0, unroll=DMA_UNROLL)
    gw = gw_ref[...]
    o_ref[...] = x_ref[...] + (gw[:, 0:1] * _load_token_tiles(rows0.at[slot], n_tok)
                               + gw[:, 1:2] * _load_token_tiles(rows1.at[slot], n_tok))


def _combine(dest, gw, x, ybuf):
    t, d = x.shape
    tc = min(DMA_TILE, t)
    steps = t // tc
    tok = pl.BlockSpec((tc, d), lambda i: (i, 0))
    slots = pltpu.VMEM((2, tc * TOKEN_ROWS, LANES), ybuf.dtype)
    return pl.pallas_call(
        _combine_kernel,
        out_shape=jax.ShapeDtypeStruct((t, d), F32),
        grid=(steps,),
        in_specs=[pl.BlockSpec((TOP_K * tc,), lambda i: (i,), memory_space=pltpu.SMEM),
                  pl.BlockSpec((TOP_K * tc,), lambda i: (jnp.minimum(i + 1, steps - 1),), memory_space=pltpu.SMEM),
                  pl.BlockSpec((tc, ROUTER_LANES), lambda i: (i, 0)), tok,
                  pl.BlockSpec(memory_space=pl.ANY)],
        out_specs=tok,
        scratch_shapes=[slots, slots, pltpu.SemaphoreType.DMA((2,))],
        compiler_params=_cparams("arbitrary"),
        name="combine",
    )(dest, dest, gw, x, ybuf)


def _block_diag(w):
    nb, bd, _ = w.shape
    eye = jnp.eye(nb, dtype=w.dtype)
    return (eye[:, None, :, None] * w[:, :, None, :]).reshape(nb * bd, nb * bd)


def _moe(x, h2, ids, gw, cnt, wg, wu, wd, layer):
    t, d = x.shape
    counts = cnt[0, EXPERT_LANE0:EXPERT_LANE0 + N_EXPERTS].astype(jnp.int32)
    padded = (counts + MOE_BLOCK - 1) // MOE_BLOCK * MOE_BLOCK
    pend = jnp.cumsum(padded)
    pstart = pend - padded
    lookup = lambda table, idx: jnp.sum(
        jnp.where(idx[..., None] == jnp.arange(table.shape[0], dtype=jnp.int32), table, 0), axis=-1)
    dest = (lookup(pstart, ids[:, 0:TOP_K]) + ids[:, TOP_K:2 * TOP_K]).reshape(t * TOP_K)
    n_rows = t * TOP_K + N_EXPERTS * MOE_BLOCK
    n_blk = n_rows // MOE_BLOCK
    blk_start = jnp.arange(n_blk, dtype=jnp.int32) * MOE_BLOCK
    blk_expert = jnp.minimum(jnp.sum(pend[None, :] <= blk_start[:, None], axis=1), N_EXPERTS - 1).astype(jnp.int32)
    n_pad = N_EXPERTS * MOE_BLOCK
    pad_cum = jnp.cumsum(padded - counts)
    seg_first = jnp.concatenate([pstart + counts, pend[-1:]])
    seg_skip = jnp.concatenate([jnp.zeros((1,), jnp.int32), pad_cum])
    j = jnp.arange(n_pad, dtype=jnp.int32)
    seg = jnp.sum(pad_cum[None, :] <= j[:, None], axis=1)
    pad_dest = (lookup(seg_first, seg) + (j - lookup(seg_skip, seg))).astype(jnp.int32)
    blocks_used = (pend[-1:] // MOE_BLOCK).astype(jnp.int32)
    xbuf = _dispatch(dest, pad_dest, h2, n_rows)
    ybuf = _experts(blk_expert, blocks_used, xbuf, wg, wu, wd, layer)
    return _combine(dest, gw, x, ybuf)


def kernel(x, norm_mix, w_in, conv_w, conv_b, lru_w_a, lru_b_a, lru_w_x, lru_b_x, lru_lambda, q_norm, k_norm, norm_out_lru, norm_out_attn, w_out, norm_ffn, router_group_w, router_group_b, router_expert_w, router_expert_b, w_gate, w_up, w_down):
    b, s, d = x.shape
    depth = w_in.shape[0]
    d_attn = norm_out_attn.shape[1]
    pad = ROUTER_LANES - N_GROUPS - N_EXPERTS
    row = lambda v: v.reshape(1, -1)
    for l in range(depth):
        y_lru, q, k, v = _in_proj(x, row(norm_mix[l]), w_in[l].astype(BF16), conv_w[l], row(conv_b[l]),
                                  _block_diag(lru_w_a[l]).astype(BF16), row(lru_b_a[l]),
                                  _block_diag(lru_w_x[l]).astype(BF16), row(lru_b_x[l]),
                                  row(lru_lambda[l]), d_attn)
        y_attn = _attention(q, k, v, row(jnp.tile(q_norm[l], HEADS_PER_SLAB)),
                            row(jnp.tile(k_norm[l], HEADS_PER_SLAB)))
        wr = jnp.pad(jnp.concatenate([router_group_w[l], router_expert_w[l]], axis=1), ((0, 0), (0, pad)))
        wr_hi = wr.astype(BF16)
        wr = jnp.concatenate([wr_hi, (wr - wr_hi.astype(F32)).astype(BF16)], axis=1)
        br = jnp.pad(jnp.concatenate([router_group_b[l], router_expert_b[l]]), (0, pad))
        xn, h2, ids, gw, cnt = _out_proj(x, y_lru, y_attn, row(norm_out_lru[l]), row(norm_out_attn[l]),
                                         w_out[l].astype(BF16), row(norm_ffn[l]), wr, row(br))
        t = b * s
        x = _moe(xn.reshape(t, d), h2, ids.reshape(t, ROUTER_LANES), gw.reshape(t, ROUTER_LANES), cnt,
                 w_gate, w_up, w_down, l).reshape(b, s, d)
    return x
```

```python
import functools

import jax
import jax.numpy as jnp
from jax import lax
from jax.experimental import pallas as pl
from jax.experimental.pallas import tpu as pltpu

F32 = jnp.float32
BF16 = jnp.bfloat16

EPS = 1e-6
LANES = 128
HEAD_DIM = 64
HEADS_PER_SLAB = LANES // HEAD_DIM
ATTN_BLOCK = 128
DILATIONS = (1, 4, 16)
CONV_WIDTH = 4
LRU_C = 8.0
N_GROUPS = 4
EXPERTS_PER_GROUP = 8
N_EXPERTS = N_GROUPS * EXPERTS_PER_GROUP
TOP_K = 2
ROUTER_LANES = LANES
EXPERT_LANE0 = N_GROUPS
ROUTER_ROWS = 40
VMEM_LIMIT = 56 * 1024 * 1024

TM_PROJ = 512
SCAN_GROUP = 8
PAD_GROUP = 16
PAD_PITCH = 24
ATTN_UNROLL = 8
NORM_CHUNK = 512
MOE_BLOCK = 512
TOKEN_ROWS = 4
DMA_TILE = 1024
DISPATCH_TILE = 1024
DMA_UNROLL = 8


def _rms(x, g):
    return x * lax.rsqrt(jnp.mean(x * x, axis=-1, keepdims=True) + EPS) * g


def _log1p(z):
    w = 1.0 + z
    return jnp.where(w == 1.0, z, z * jnp.log(w) / jnp.where(w == 1.0, 1.0, w - 1.0))


def _softplus(x):
    return jnp.maximum(x, 0.0) + _log1p(jnp.exp(-jnp.abs(x)))


def _store_token_tiles(ref, val):
    n, d = val.shape
    assert d == 2 * TOKEN_ROWS * LANES
    for r in range(TOKEN_ROWS):
        hi = val[:, r * LANES:(r + 1) * LANES]
        lo = val[:, d // 2 + r * LANES:d // 2 + (r + 1) * LANES]
        hi = lax.bitcast_convert_type(hi.astype(BF16).astype(F32), jnp.uint32)
        lo = lax.bitcast_convert_type(lo.astype(BF16).astype(F32), jnp.uint32)
        ref[pl.ds(r, n, stride=TOKEN_ROWS), :] = hi | (lo >> 16)


def _load_token_tiles(ref, n):
    words = [ref[pl.ds(r, n, stride=TOKEN_ROWS), :] for r in range(TOKEN_ROWS)]
    hi = [lax.bitcast_convert_type(w & jnp.uint32(0xFFFF0000), F32) for w in words]
    lo = [lax.bitcast_convert_type(w << 16, F32) for w in words]
    return jnp.concatenate(hi + lo, axis=-1)


def _cparams(*sem):
    return pltpu.CompilerParams(dimension_semantics=sem, vmem_limit_bytes=VMEM_LIMIT)


def _in_proj_kernel(x_ref, g_ref, w_ref, cw_ref, cb_ref, wa_ref, ba_ref, wx_ref, bx_ref, lam_ref,
                    y_ref, q_ref, k_ref, v_ref, tail_scr, h_scr, *, d_lru, d_attn):
    h = _rms(x_ref[0], g_ref[...]).astype(BF16)

    def cols(lo, n):
        return jnp.dot(h, w_ref[:, lo:lo + n], preferred_element_type=F32)

    @pl.when(pl.program_id(1) == 0)
    def _():
        tail_scr[...] = jnp.zeros_like(tail_scr)
        h_scr[...] = jnp.zeros_like(h_scr)

    xl, gate = cols(0, d_lru), cols(d_lru, d_lru)
    for grp in range(d_lru // LANES):
        sl = slice(grp * LANES, (grp + 1) * LANES)
        y_ref[0, :, sl] = _rg_lru_tile(
            xl[:, sl], gate[:, sl], cw_ref[:, sl], cb_ref[:, sl], wa_ref[grp], ba_ref[:, sl],
            wx_ref[grp], bx_ref[:, sl], lam_ref[:, sl], tail_scr.at[grp], h_scr.at[grp])
    base = 2 * d_lru
    for ref in (q_ref, k_ref, v_ref):
        heads = cols(base, d_attn)
        for j in range(d_attn // LANES):
            ref[0, j] = heads[:, j * LANES:(j + 1) * LANES]
        base += d_attn


def _in_proj(x, g, w, cw, cb, wa, ba, wx, bx, lam, d_attn):
    b, s, d = x.shape
    d_lru = cw.shape[1]
    tm = min(TM_PROJ, s)
    n_slab = d_attn // LANES
    slab = jax.ShapeDtypeStruct((b, n_slab, s, LANES), F32)
    slab_spec = pl.BlockSpec((1, n_slab, tm, LANES), lambda i, j: (i, 0, j, 0))
    const = lambda shape: pl.BlockSpec(shape, lambda i, j: (0,) * len(shape))
    vec = const((1, d_lru))
    return pl.pallas_call(
        functools.partial(_in_proj_kernel, d_lru=d_lru, d_attn=d_attn),
        out_shape=(jax.ShapeDtypeStruct((b, s, d_lru), F32), slab, slab, slab),
        grid=(b, s // tm),
        in_specs=[pl.BlockSpec((1, tm, d), lambda i, j: (i, j, 0)), const((1, d)), const(w.shape),
                  const((CONV_WIDTH, d_lru)), vec, const(wa.shape), vec, const(wx.shape), vec, vec],
        out_specs=(pl.BlockSpec((1, tm, d_lru), lambda i, j: (i, j, 0)), slab_spec, slab_spec, slab_spec),
        scratch_shapes=[pltpu.VMEM((d_lru // LANES, 8, LANES), F32), pltpu.VMEM((d_lru // LANES, 1, LANES), F32)],
        compiler_params=_cparams("parallel", "arbitrary"),
        name="in_proj",
    )(x, g, w, cw, cb, wa, ba, wx, bx, lam)


def _rg_lru_tile(x, gate, cw, cb, wa, ba, wx, bx, lam, tail_scr, h_scr):
    tc = x.shape[0]
    halo = tail_scr.shape[0]
    tail = tail_scr[...]
    first = lax.broadcasted_iota(jnp.int32, (halo, 1), 0)
    xc = cb
    for j in range(CONV_WIDTH):
        back = CONV_WIDTH - 1 - j
        if back == 0:
            xs = x
        else:
            shifted = pltpu.roll(x, back, axis=0)
            head = jnp.where(first < back, pltpu.roll(tail, back, axis=0), shifted[0:halo])
            xs = jnp.concatenate([head, shifted[halo:]], axis=0)
        xc = xc + cw[j:j + 1, :] * xs
    tail_scr[...] = x[tc - halo:tc, :]

    xb = xc.astype(BF16)
    r = jax.nn.sigmoid(jnp.dot(xb, wa, preferred_element_type=F32) + ba)
    i = jax.nn.sigmoid(jnp.dot(xb, wx, preferred_element_type=F32) + bx)
    log_a = -LRU_C * r * _softplus(-lam)
    a = jnp.exp(log_a)
    u = jnp.sqrt(jnp.tanh(-log_a) * (1.0 + a * a)) * (i * xc)

    in_group = lax.broadcasted_iota(jnp.int32, (tc, 1), 0) & (SCAN_GROUP - 1)
    shift = 1
    while shift < SCAN_GROUP:
        keep = in_group >= shift
        a_prev = pltpu.roll(a, shift, axis=0)
        u_prev = pltpu.roll(u, shift, axis=0)
        u = jnp.where(keep, a * u_prev + u, u)
        a = jnp.where(keep, a * a_prev, a)
        shift *= 2
    h_last = h_scr[...]
    groups = []
    for g in range(tc // SCAN_GROUP):
        rows = slice(g * SCAN_GROUP, (g + 1) * SCAN_GROUP)
        h_g = a[rows] * h_last + u[rows]
        h_last = h_g[SCAN_GROUP - 1:SCAN_GROUP, :]
        groups.append(h_g)
    h_scr[...] = h_last
    return jnp.concatenate(groups, axis=0) * jax.nn.gelu(gate)


def _attn_kernel(q_ref, k_ref, v_ref, qg_ref, kg_ref, o_ref,
                 qn, kn, m_s, l_s, acc_s, qp, kp, vp, m_p, l_p, acc_p):
    s_len = q_ref.shape[2]
    blk = ATTN_BLOCK
    lane = lax.broadcasted_iota(jnp.int32, (1, LANES), 1)
    lo = lane < HEAD_DIM
    n_chunk = s_len // NORM_CHUNK
    groups = NORM_CHUNK // PAD_GROUP

    def padded_base(c):
        return pl.multiple_of(c * (groups * PAD_PITCH), 8)

    def to_padded(dst, c, val):
        for g in range(groups):
            dst[pl.ds(padded_base(c) + g * PAD_PITCH, PAD_GROUP), :] = val[g * PAD_GROUP:(g + 1) * PAD_GROUP]

    def from_padded(src, c):
        return jnp.concatenate(
            [src[pl.ds(padded_base(c) + g * PAD_PITCH, PAD_GROUP), :] for g in range(groups)], axis=0)

    same_head = (lax.broadcasted_iota(jnp.int32, (LANES, LANES), 0) // HEAD_DIM
                 == lax.broadcasted_iota(jnp.int32, (LANES, LANES), 1) // HEAD_DIM)
    head_mean = jnp.where(same_head, 1.0 / HEAD_DIM, 0.0).astype(BF16)

    def norm_body(c, carry):
        rows = pl.ds(pl.multiple_of(c * NORM_CHUNK, NORM_CHUNK), NORM_CHUNK)
        for src, g_ref, dst, dst_p, scale in ((q_ref, qg_ref, qn, qp, HEAD_DIM ** -0.5),
                                              (k_ref, kg_ref, kn, kp, 1.0)):
            x = src[0, 0, rows, :]
            x2 = x * x
            x2_hi = x2.astype(BF16)
            x2_lo = (x2 - x2_hi.astype(F32)).astype(BF16)
            ms = (jnp.dot(x2_hi, head_mean, preferred_element_type=F32)
                  + jnp.dot(x2_lo, head_mean, preferred_element_type=F32))
            y = x * lax.rsqrt(ms + EPS) * g_ref[...] * scale
            dst[rows, :] = y
            to_padded(dst_p, c, y)
        to_padded(vp, c, v_ref[0, 0, rows, :])
        return carry

    lax.fori_loop(0, n_chunk, norm_body, 0)

    rows2 = lax.broadcasted_iota(jnp.int32, (HEADS_PER_SLAB * blk, blk), 0)
    qi = rows2 & (blk - 1)
    kj = lax.broadcasted_iota(jnp.int32, (HEADS_PER_SLAB * blk, blk), 1)
    band_prev = kj >= qi
    band_cur = kj <= qi
    nt = (((1,), (1,)), ((), ()))

    def branch(d, q_src, k_src, v_src, stats, merge):
        m_r, l_r, acc_r = stats
        log_d = d.bit_length() - 1
        padded = d == PAD_GROUP
        pitch = PAD_PITCH if padded else d
        span = blk * pitch

        def sl(start):
            return pl.ds(start, blk) if pitch == 1 else pl.ds(start, blk, stride=pitch)

        def body(it, carry):
            n = it >> log_d
            r = it & (d - 1)
            cur = sl(n * span + r)
            prev = sl(jnp.maximum(n - 1, 0) * span + r)
            qt = q_src(cur)
            qs = jnp.concatenate([jnp.where(lo, qt, 0.0), jnp.where(lo, 0.0, qt)], axis=0).astype(BF16)
            s_c = lax.dot_general(qs, k_src(cur).astype(BF16), nt, preferred_element_type=F32)
            s_p = lax.dot_general(qs, k_src(prev).astype(BF16), nt, preferred_element_type=F32)
            s_c = jnp.where(band_cur, s_c, -jnp.inf)
            s_p = jnp.where(band_prev & (n > 0), s_p, -jnp.inf)
            m = jnp.max(jnp.maximum(s_c, s_p), axis=-1, keepdims=True)
            p_c = jnp.exp(s_c - m)
            p_p = jnp.exp(s_p - m)
            l = jnp.sum(p_c + p_p, axis=-1, keepdims=True)
            acc = (jnp.dot(p_c.astype(BF16), v_src(cur).astype(BF16), preferred_element_type=F32)
                   + jnp.dot(p_p.astype(BF16), v_src(prev).astype(BF16), preferred_element_type=F32))
            acc_t = jnp.where(lo, acc[:blk], acc[blk:])
            m_t = jnp.where(lo, jnp.broadcast_to(m[:blk], (blk, LANES)), jnp.broadcast_to(m[blk:], (blk, LANES)))
            l_t = jnp.where(lo, jnp.broadcast_to(l[:blk], (blk, LANES)), jnp.broadcast_to(l[blk:], (blk, LANES)))
            if merge:
                m_o = m_r[cur, :]
                m_n = jnp.maximum(m_o, m_t)
                c_o = jnp.exp(m_o - m_n)
                c_t = jnp.exp(m_t - m_n)
                m_r[cur, :] = m_n
                l_r[cur, :] = l_r[cur, :] * c_o + l_t * c_t
                acc_r[cur, :] = acc_r[cur, :] * c_o + acc_t * c_t
            else:
                m_r[cur, :] = m_t
                l_r[cur, :] = l_t
                acc_r[cur, :] = acc_t
            return carry

        lax.fori_loop(0, s_len // blk, body, 0, unroll=ATTN_UNROLL)

    token_order = (lambda s: qn[s, :], lambda s: kn[s, :], lambda s: v_ref[0, 0, s, :], (m_s, l_s, acc_s))
    padded_order = (lambda s: qp[s, :], lambda s: kp[s, :], lambda s: vp[s, :], (m_p, l_p, acc_p))
    for idx, d in enumerate(DILATIONS):
        if d == PAD_GROUP:
            branch(d, *padded_order, merge=False)
        else:
            branch(d, *token_order, merge=idx > 0)

    def out_body(c, carry):
        rows = pl.ds(pl.multiple_of(c * NORM_CHUNK, NORM_CHUNK), NORM_CHUNK)
        m_a, m_b = m_s[rows, :], from_padded(m_p, c)
        m_n = jnp.maximum(m_a, m_b)
        c_a = jnp.exp(m_a - m_n)
        c_b = jnp.exp(m_b - m_n)
        o_ref[0, 0, rows, :] = ((acc_s[rows, :] * c_a + from_padded(acc_p, c) * c_b)
                                / (l_s[rows, :] * c_a + from_padded(l_p, c) * c_b))
        return carry

    lax.fori_loop(0, n_chunk, out_body, 0)


def _attention(q, k, v, qg, kg):
    b, n_slab, s, _ = q.shape
    assert s % (ATTN_BLOCK * max(DILATIONS)) == 0 and s % NORM_CHUNK == 0 and max(DILATIONS) == PAD_GROUP
    slab_spec = pl.BlockSpec((1, 1, s, LANES), lambda i, j: (i, j, 0, 0))
    vec = pl.BlockSpec((1, LANES), lambda i, j: (0, 0))
    return pl.pallas_call(
        _attn_kernel,
        out_shape=jax.ShapeDtypeStruct(q.shape, F32),
        grid=(b, n_slab),
        in_specs=[slab_spec, slab_spec, slab_spec, vec, vec],
        out_specs=slab_spec,
        scratch_shapes=([pltpu.VMEM((s, LANES), F32) for _ in range(5)]
                        + [pltpu.VMEM((s // PAD_GROUP * PAD_PITCH, LANES), F32) for _ in range(6)]),
        compiler_params=_cparams("parallel", "parallel"),
        name="dilated_attn",
    )(q, k, v, qg, kg)


def _out_proj_kernel(x_ref, yl_ref, ya_ref, gl_ref, ga_ref, w_ref, gf_ref, wr_ref, br_ref,
                     xn_ref, h2_ref, ids_ref, gw_ref, cnt_ref, carry):
    d_lru = yl_ref.shape[2]
    nl = _rms(yl_ref[0], gl_ref[...]).astype(BF16)
    ya = jnp.concatenate([ya_ref[0, j] for j in range(ya_ref.shape[1])], axis=-1)
    na = _rms(ya, ga_ref[...]).astype(BF16)
    xn = x_ref[0] + (jnp.dot(nl, w_ref[0:d_lru, :], preferred_element_type=F32)
                     + jnp.dot(na, w_ref[d_lru:, :], preferred_element_type=F32))
    xn_ref[0] = xn
    h2 = _rms(xn, gf_ref[...])
    _store_token_tiles(h2_ref, h2)
    hi = h2.astype(BF16)
    lo = (h2 - hi.astype(F32)).astype(BF16)
    both = jnp.dot(hi, wr_ref[...], preferred_element_type=F32)
    logits = (both[:, :ROUTER_LANES] + both[:, ROUTER_LANES:]
              + jnp.dot(lo, wr_ref[:, :ROUTER_LANES], preferred_element_type=F32) + br_ref[...])

    @pl.when((pl.program_id(0) == 0) & (pl.program_id(1) == 0))
    def _():
        carry[...] = jnp.zeros_like(carry)

    ids_ref[0], gw_ref[0] = _route_tile(logits, carry)
    cnt_ref[...] = carry[...]


def _out_proj(x, yl, ya, gl, ga, w, gf, wr, br):
    b, s, d = x.shape
    d_lru = yl.shape[2]
    n_slab = ya.shape[1]
    tm = min(TM_PROJ, s)
    nj = s // tm
    tok = pl.BlockSpec((1, tm, d), lambda i, j: (i, j, 0))
    tiles = pl.BlockSpec((tm * TOKEN_ROWS, LANES), lambda i, j: (i * nj + j, 0))
    lanes = pl.BlockSpec((1, tm, ROUTER_LANES), lambda i, j: (i, j, 0))
    const = lambda shape: pl.BlockSpec(shape, lambda i, j: (0,) * len(shape))
    return pl.pallas_call(
        _out_proj_kernel,
        out_shape=(jax.ShapeDtypeStruct((b, s, d), F32),
                   jax.ShapeDtypeStruct((b * s * TOKEN_ROWS, LANES), jnp.uint32),
                   jax.ShapeDtypeStruct((b, s, ROUTER_LANES), jnp.int32),
                   jax.ShapeDtypeStruct((b, s, ROUTER_LANES), F32),
                   jax.ShapeDtypeStruct((ROUTER_ROWS, LANES), F32)),
        grid=(b, s // tm),
        in_specs=[tok, pl.BlockSpec((1, tm, d_lru), lambda i, j: (i, j, 0)),
                  pl.BlockSpec((1, n_slab, tm, LANES), lambda i, j: (i, 0, j, 0)),
                  const((1, d_lru)), const((1, n_slab * LANES)), const(w.shape), const((1, d)),
                  const(wr.shape), const((1, ROUTER_LANES))],
        out_specs=(tok, tiles, lanes, lanes, const((ROUTER_ROWS, LANES))),
        scratch_shapes=[pltpu.VMEM((ROUTER_ROWS, LANES), F32)],
        compiler_params=_cparams("arbitrary", "arbitrary"),
        name="out_proj",
    )(x, yl, ya, gl, ga, w, gf, wr, br)


def _route_tile(lg, carry):
    tr = lg.shape[0]
    lt = lg.T[:ROUTER_ROWS]
    row = lax.broadcasted_iota(jnp.int32, (ROUTER_ROWS, tr), 0)
    big = jnp.int32(ROUTER_ROWS)

    def argmax(vals):
        top = jnp.max(vals, axis=0, keepdims=True)
        return top, jnp.min(jnp.where(vals == top, row, big), axis=0, keepdims=True)

    g_logit = jnp.where(row < N_GROUPS, lt, -jnp.inf)
    g_top, g_idx = argmax(g_logit)
    p_top = 1.0 / jnp.sum(jnp.exp(g_logit - g_top), axis=0, keepdims=True)
    e_lo = EXPERT_LANE0 + g_idx * EXPERTS_PER_GROUP
    e_logit = jnp.where((row >= e_lo) & (row < e_lo + EXPERTS_PER_GROUP), lt, -jnp.inf)
    v1, i1 = argmax(e_logit)
    v2, i2 = argmax(jnp.where(row == i1, -jnp.inf, e_logit))
    e21 = jnp.exp(v2 - v1)
    w1 = 1.0 / (1.0 + e21) * p_top
    w2 = e21 / (1.0 + e21) * p_top

    hot = ((row == i1) | (row == i2))
    rr = lax.broadcasted_iota(jnp.int32, (tr, tr), 0)
    cc = lax.broadcasted_iota(jnp.int32, (tr, tr), 1)
    earlier = (rr < cc).astype(BF16)
    count = carry[0:ROUTER_ROWS, 0:1]
    prefix = jnp.dot(hot.astype(BF16), earlier, preferred_element_type=F32) + count
    rank1 = jnp.sum(jnp.where(row == i1, prefix, 0.0), axis=0, keepdims=True).astype(jnp.int32)
    rank2 = jnp.sum(jnp.where(row == i2, prefix, 0.0), axis=0, keepdims=True).astype(jnp.int32)
    carry[0:ROUTER_ROWS, :] = carry[0:ROUTER_ROWS, :] + jnp.sum(hot.astype(F32), axis=1, keepdims=True)

    out_row = lax.broadcasted_iota(jnp.int32, (ROUTER_LANES, tr), 0)
    ids_t = jnp.where(out_row == 0, i1 - EXPERT_LANE0,
                      jnp.where(out_row == 1, i2 - EXPERT_LANE0,
                                jnp.where(out_row == 2, rank1, jnp.where(out_row == 3, rank2, 0))))
    gw_t = jnp.where(out_row == 0, w1, jnp.where(out_row == 1, w2, 0.0))
    return ids_t.T, gw_t.T


def _row_copy(src_ref, src_row, dst_ref, dst_row, sem):
    rows = lambda tok: pl.ds(pl.multiple_of(tok * TOKEN_ROWS, TOKEN_ROWS), TOKEN_ROWS)
    return pltpu.make_async_copy(src_ref.at[rows(src_row)], dst_ref.at[rows(dst_row)], sem)


def _dispatch_kernel(dest_ref, pad_ref, h2_ref, xbuf_hbm, zero_s, sem):
    n_tok = h2_ref.shape[0] // TOKEN_ROWS
    n_pad = pad_ref.shape[0]
    zero_s[...] = jnp.zeros_like(zero_s)

    def start(i, carry):
        for k in range(TOP_K):
            _row_copy(h2_ref, i, xbuf_hbm, dest_ref[TOP_K * i + k], sem).start(priority=k)
        return carry

    def start_pad(i, carry):
        _row_copy(zero_s, 0, xbuf_hbm, pad_ref[i], sem).start()
        return carry

    def wait(i, carry):
        _row_copy(h2_ref, 0, xbuf_hbm, 0, sem).wait()
        return carry

    lax.fori_loop(0, n_tok, start, 0, unroll=DMA_UNROLL)
    lax.fori_loop(0, n_pad, start_pad, 0, unroll=DMA_UNROLL)
    lax.fori_loop(0, TOP_K * n_tok + n_pad, wait, 0, unroll=DMA_UNROLL)


def _dispatch(dest, pad_dest, h2, n_rows):
    t = h2.shape[0] // TOKEN_ROWS
    td = min(DISPATCH_TILE, t)
    steps = t // td
    assert pad_dest.shape[0] % steps == 0 and t * TOP_K + pad_dest.shape[0] == n_rows
    return pl.pallas_call(
        _dispatch_kernel,
        out_shape=jax.ShapeDtypeStruct((n_rows * TOKEN_ROWS, LANES), jnp.uint32),
        grid=(steps,),
        in_specs=[pl.BlockSpec((TOP_K * td,), lambda i: (i,), memory_space=pltpu.SMEM),
                  pl.BlockSpec((pad_dest.shape[0] // steps,), lambda i: (i,), memory_space=pltpu.SMEM),
                  pl.BlockSpec((td * TOKEN_ROWS, LANES), lambda i: (i, 0))],
        out_specs=pl.BlockSpec(memory_space=pl.ANY),
        scratch_shapes=[pltpu.VMEM((TOKEN_ROWS, LANES), jnp.uint32), pltpu.SemaphoreType.DMA],
        compiler_params=_cparams("arbitrary"),
        name="dispatch",
    )(dest, pad_dest, h2)


def _expert_kernel(be_ref, used_ref, x_ref, wg_ref, wu_ref, wd_ref, y_ref, wg_s, wu_s, wd_s):
    i = pl.program_id(0)
    new_expert = (i == 0) | (be_ref[i] != be_ref[jnp.maximum(i - 1, 0)])
    in_use = i < used_ref[0]

    @pl.when(new_expert & in_use)
    def _():
        wg_s[...] = wg_ref[0, 0].astype(BF16)
        wu_s[...] = wu_ref[0, 0].astype(BF16)
        wd_s[...] = wd_ref[0, 0].astype(BF16)

    @pl.when(in_use)
    def _():
        x = _load_token_tiles(x_ref, MOE_BLOCK).astype(BF16)
        g = jnp.dot(x, wg_s[...], preferred_element_type=F32)
        u = jnp.dot(x, wu_s[...], preferred_element_type=F32)
        hid = (jax.nn.silu(g) * u).astype(BF16)
        _store_token_tiles(y_ref, jnp.dot(hid, wd_s[...], preferred_element_type=F32))

    @pl.when(jnp.logical_not(in_use))
    def _():
        y_ref[...] = jnp.zeros_like(y_ref)


def _experts(blk_expert, blocks_used, xbuf, wg, wu, wd, layer):
    d, de = wg.shape[2], wg.shape[3]
    n_blk = xbuf.shape[0] // (MOE_BLOCK * TOKEN_ROWS)
    rows = pl.BlockSpec((MOE_BLOCK * TOKEN_ROWS, LANES), lambda i, be, used: (i, 0))
    w_in = pl.BlockSpec((1, 1, d, de), lambda i, be, used: (layer, be[i], 0, 0))
    w_out = pl.BlockSpec((1, 1, de, d), lambda i, be, used: (layer, be[i], 0, 0))
    return pl.pallas_call(
        _expert_kernel,
        out_shape=jax.ShapeDtypeStruct(xbuf.shape, xbuf.dtype),
        grid_spec=pltpu.PrefetchScalarGridSpec(
            num_scalar_prefetch=2,
            grid=(n_blk,),
            in_specs=[rows, w_in, w_in, w_out],
            out_specs=rows,
            scratch_shapes=[pltpu.VMEM((d, de), BF16), pltpu.VMEM((d, de), BF16), pltpu.VMEM((de, d), BF16)]),
        compiler_params=_cparams("arbitrary"),
        name="experts",
    )(blk_expert, blocks_used, xbuf, wg, wu, wd)


def _combine_kernel(dest_ref, dest_next_ref, gw_ref, x_ref, ybuf_hbm, o_ref, rows0, rows1, sems):
    n_tok = x_ref.shape[0]
    step = pl.program_id(0)
    slot = step & 1

    def issue(idx_ref, into):
        def start(i, carry):
            for k, rows in enumerate((rows0, rows1)):
                _row_copy(ybuf_hbm, idx_ref[TOP_K * i + k], rows.at[into], i, sems.at[into]).start(priority=k)
            return carry
        lax.fori_loop(0, n_tok, start, 0, unroll=DMA_UNROLL)

    @pl.when(step == 0)
    def _():
        issue(dest_ref, 0)

    @pl.when(step + 1 < pl.num_programs(0))
    def _():
        issue(dest_next_ref, 1 - slot)

    def wait(i, carry):
        for rows in (rows0, rows1):
            _row_copy(ybuf_hbm, 0, rows.at[slot], 0, sems.at[slot]).wait()
        return carry

    lax.fori_loop(0, n_tok, wait, 0, unroll=DMA_UNROLL)
    gw = gw_ref[...]
    o_ref[...] = x_ref[...] + (gw[:, 0:1] * _load_token_tiles(rows0.at[slot], n_tok)
                               + gw[:, 1:2] * _load_token_tiles(rows1.at[slot], n_tok))


def _combine(dest, gw, x, ybuf):
    t, d = x.shape
    tc = min(DMA_TILE, t)
    steps = t // tc
    tok = pl.BlockSpec((tc, d), lambda i: (i, 0))
    slots = pltpu.VMEM((2, tc * TOKEN_ROWS, LANES), ybuf.dtype)
    return pl.pallas_call(
        _combine_kernel,
        out_shape=jax.ShapeDtypeStruct((t, d), F32),
        grid=(steps,),
        in_specs=[pl.BlockSpec((TOP_K * tc,), lambda i: (i,), memory_space=pltpu.SMEM),
                  pl.BlockSpec((TOP_K * tc,), lambda i: (jnp.minimum(i + 1, steps - 1),), memory_space=pltpu.SMEM),
                  pl.BlockSpec((tc, ROUTER_LANES), lambda i: (i, 0)), tok,
                  pl.BlockSpec(memory_space=pl.ANY)],
        out_specs=tok,
        scratch_shapes=[slots, slots, pltpu.SemaphoreType.DMA((2,))],
        compiler_params=_cparams("arbitrary"),
        name="combine",
    )(dest, dest, gw, x, ybuf)


def _block_diag(w):
    nb, bd, _ = w.shape
    per = LANES // bd
    w = w.reshape(nb // per, per, bd, bd)
    eye = jnp.eye(per, dtype=w.dtype)
    return (eye[None, :, None, :, None] * w[:, :, :, None, :]).reshape(nb // per, LANES, LANES)


def _moe(x, h2, ids, gw, cnt, wg, wu, wd, layer):
    t, d = x.shape
    counts = cnt[EXPERT_LANE0:EXPERT_LANE0 + N_EXPERTS, 0].astype(jnp.int32)
    padded = (counts + MOE_BLOCK - 1) // MOE_BLOCK * MOE_BLOCK
    pend = jnp.cumsum(padded)
    pstart = pend - padded
    lookup = lambda table, idx: jnp.sum(
        jnp.where(idx[..., None] == jnp.arange(table.shape[0], dtype=jnp.int32), table, 0), axis=-1)
    dest = (lookup(pstart, ids[:, 0:TOP_K]) + ids[:, TOP_K:2 * TOP_K]).reshape(t * TOP_K)
    n_rows = t * TOP_K + N_EXPERTS * MOE_BLOCK
    n_blk = n_rows // MOE_BLOCK
    blk_start = jnp.arange(n_blk, dtype=jnp.int32) * MOE_BLOCK
    blk_expert = jnp.minimum(jnp.sum(pend[None, :] <= blk_start[:, None], axis=1), N_EXPERTS - 1).astype(jnp.int32)
    n_pad = N_EXPERTS * MOE_BLOCK
    pad_cum = jnp.cumsum(padded - counts)
    seg_first = jnp.concatenate([pstart + counts, pend[-1:]])
    seg_skip = jnp.concatenate([jnp.zeros((1,), jnp.int32), pad_cum])
    j = jnp.arange(n_pad, dtype=jnp.int32)
    seg = jnp.sum(pad_cum[None, :] <= j[:, None], axis=1)
    pad_dest = (lookup(seg_first, seg) + (j - lookup(seg_skip, seg))).astype(jnp.int32)
    blocks_used = (pend[-1:] // MOE_BLOCK).astype(jnp.int32)
    xbuf = _dispatch(dest, pad_dest, h2, n_rows)
    ybuf = _experts(blk_expert, blocks_used, xbuf, wg, wu, wd, layer)
    return _combine(dest, gw, x, ybuf)


def kernel(x, norm_mix, w_in, conv_w, conv_b, lru_w_a, lru_b_a, lru_w_x, lru_b_x, lru_lambda, q_norm, k_norm, norm_out_lru, norm_out_attn, w_out, norm_ffn, router_group_w, router_group_b, router_expert_w, router_expert_b, w_gate, w_up, w_down):
    b, s, d = x.shape
    depth = w_in.shape[0]
    d_attn = norm_out_attn.shape[1]
    pad = ROUTER_LANES - N_GROUPS - N_EXPERTS
    row = lambda v: v.reshape(1, -1)
    for l in range(depth):
        y_lru, q, k, v = _in_proj(x, row(norm_mix[l]), w_in[l].astype(BF16), conv_w[l], row(conv_b[l]),
                                  _block_diag(lru_w_a[l]).astype(BF16), row(lru_b_a[l]),
                                  _block_diag(lru_w_x[l]).astype(BF16), row(lru_b_x[l]),
                                  row(lru_lambda[l]), d_attn)
        y_attn = _attention(q, k, v, row(jnp.tile(q_norm[l], HEADS_PER_SLAB)),
                            row(jnp.tile(k_norm[l], HEADS_PER_SLAB)))
        wr = jnp.pad(jnp.concatenate([router_group_w[l], router_expert_w[l]], axis=1), ((0, 0), (0, pad)))
        wr_hi = wr.astype(BF16)
        wr = jnp.concatenate([wr_hi, (wr - wr_hi.astype(F32)).astype(BF16)], axis=1)
        br = jnp.pad(jnp.concatenate([router_group_b[l], router_expert_b[l]]), (0, pad))
        xn, h2, ids, gw, cnt = _out_proj(x, y_lru, y_attn, row(norm_out_lru[l]), row(norm_out_attn[l]),
                                         w_out[l].astype(BF16), row(norm_ffn[l]), wr, row(br))
        t = b * s
        x = _moe(xn.reshape(t, d), h2, ids.reshape(t, ROUTER_LANES), gw.reshape(t, ROUTER_LANES), cnt,
                 w_gate, w_up, w_down, l).reshape(b, s, d)
    return x
```

```python
import functools

import jax
import jax.numpy as jnp
from jax import lax
from jax.experimental import pallas as pl
from jax.experimental.pallas import tpu as pltpu

F32 = jnp.float32
BF16 = jnp.bfloat16

EPS = 1e-6
LANES = 128
HEAD_DIM = 64
HEADS_PER_SLAB = LANES // HEAD_DIM
ATTN_BLOCK = 128
DILATIONS = (1, 4, 16)
CONV_WIDTH = 4
LRU_C = 8.0
N_GROUPS = 4
EXPERTS_PER_GROUP = 8
N_EXPERTS = N_GROUPS * EXPERTS_PER_GROUP
TOP_K = 2
ROUTER_LANES = LANES
EXPERT_LANE0 = N_GROUPS
ROUTER_ROWS = 40
VMEM_LIMIT = 56 * 1024 * 1024

TM_PROJ = 512
SCAN_GROUP = 8
PAD_GROUP = 16
PAD_PITCH = 24
ATTN_UNROLL = 8
NORM_CHUNK = 512
MOE_BLOCK = 512
TOKEN_ROWS = 4
DMA_TILE = 1024
DISPATCH_TILE = 1024
DMA_UNROLL = 8


def _rms(x, g):
    return x * lax.rsqrt(jnp.mean(x * x, axis=-1, keepdims=True) + EPS) * g


def _log1p(z):
    w = 1.0 + z
    return jnp.where(w == 1.0, z, z * jnp.log(w) / jnp.where(w == 1.0, 1.0, w - 1.0))


def _softplus(x):
    return jnp.maximum(x, 0.0) + _log1p(jnp.exp(-jnp.abs(x)))


def _store_token_tiles(ref, val):
    n, d = val.shape
    assert d == 2 * TOKEN_ROWS * LANES
    for r in range(TOKEN_ROWS):
        hi = val[:, r * LANES:(r + 1) * LANES]
        lo = val[:, d // 2 + r * LANES:d // 2 + (r + 1) * LANES]
        hi = lax.bitcast_convert_type(hi.astype(BF16).astype(F32), jnp.uint32)
        lo = lax.bitcast_convert_type(lo.astype(BF16).astype(F32), jnp.uint32)
        ref[pl.ds(r, n, stride=TOKEN_ROWS), :] = hi | (lo >> 16)


def _load_token_tiles(ref, n):
    words = [ref[pl.ds(r, n, stride=TOKEN_ROWS), :] for r in range(TOKEN_ROWS)]
    hi = [lax.bitcast_convert_type(w & jnp.uint32(0xFFFF0000), F32) for w in words]
    lo = [lax.bitcast_convert_type(w << 16, F32) for w in words]
    return jnp.concatenate(hi + lo, axis=-1)


def _cparams(*sem):
    return pltpu.CompilerParams(dimension_semantics=sem, vmem_limit_bytes=VMEM_LIMIT)


def _in_proj_kernel(x_ref, g_ref, w_ref, cw_ref, cb_ref, wa_ref, ba_ref, wx_ref, bx_ref, lam_ref,
                    y_ref, q_ref, k_ref, v_ref, tail_scr, h_scr, *, d_lru, d_attn):
    h = _rms(x_ref[0], g_ref[...]).astype(BF16)

    def cols(lo, n):
        return jnp.dot(h, w_ref[:, lo:lo + n], preferred_element_type=F32)

    @pl.when(pl.program_id(1) == 0)
    def _():
        tail_scr[...] = jnp.zeros_like(tail_scr)
        h_scr[...] = jnp.zeros_like(h_scr)

    xl, gate = cols(0, d_lru), cols(d_lru, d_lru)
    for grp in range(d_lru // LANES):
        sl = slice(grp * LANES, (grp + 1) * LANES)
        y_ref[0, :, sl] = _rg_lru_tile(
            xl[:, sl], gate[:, sl], cw_ref[:, sl], cb_ref[:, sl], wa_ref[grp], ba_ref[:, sl],
            wx_ref[grp], bx_ref[:, sl], lam_ref[:, sl], tail_scr.at[grp], h_scr.at[grp])
    base = 2 * d_lru
    for ref in (q_ref, k_ref, v_ref):
        heads = cols(base, d_attn)
        for j in range(d_attn // LANES):
            ref[0, j] = heads[:, j * LANES:(j + 1) * LANES]
        base += d_attn


def _in_proj(x, g, w, cw, cb, wa, ba, wx, bx, lam, d_attn):
    b, s, d = x.shape
    d_lru = cw.shape[1]
    tm = min(TM_PROJ, s)
    n_slab = d_attn // LANES
    slab = jax.ShapeDtypeStruct((b, n_slab, s, LANES), F32)
    slab_spec = pl.BlockSpec((1, n_slab, tm, LANES), lambda i, j: (i, 0, j, 0))
    const = lambda shape: pl.BlockSpec(shape, lambda i, j: (0,) * len(shape))
    vec = const((1, d_lru))
    return pl.pallas_call(
        functools.partial(_in_proj_kernel, d_lru=d_lru, d_attn=d_attn),
        out_shape=(jax.ShapeDtypeStruct((b, s, d_lru), F32), slab, slab, slab),
        grid=(b, s // tm),
        in_specs=[pl.BlockSpec((1, tm, d), lambda i, j: (i, j, 0)), const((1, d)), const(w.shape),
                  const((CONV_WIDTH, d_lru)), vec, const(wa.shape), vec, const(wx.shape), vec, vec],
        out_specs=(pl.BlockSpec((1, tm, d_lru), lambda i, j: (i, j, 0)), slab_spec, slab_spec, slab_spec),
        scratch_shapes=[pltpu.VMEM((d_lru // LANES, 8, LANES), F32), pltpu.VMEM((d_lru // LANES, 1, LANES), F32)],
        compiler_params=_cparams("parallel", "arbitrary"),
        name="in_proj",
    )(x, g, w, cw, cb, wa, ba, wx, bx, lam)


def _rg_lru_tile(x, gate, cw, cb, wa, ba, wx, bx, lam, tail_scr, h_scr):
    tc = x.shape[0]
    halo = tail_scr.shape[0]
    tail = tail_scr[...]
    first = lax.broadcasted_iota(jnp.int32, (halo, 1), 0)
    xc = cb
    for j in range(CONV_WIDTH):
        back = CONV_WIDTH - 1 - j
        if back == 0:
            xs = x
        else:
            shifted = pltpu.roll(x, back, axis=0)
            head = jnp.where(first < back, pltpu.roll(tail, back, axis=0), shifted[0:halo])
            xs = jnp.concatenate([head, shifted[halo:]], axis=0)
        xc = xc + cw[j:j + 1, :] * xs
    tail_scr[...] = x[tc - halo:tc, :]

    xb = xc.astype(BF16)
    r = jax.nn.sigmoid(jnp.dot(xb, wa, preferred_element_type=F32) + ba)
    i = jax.nn.sigmoid(jnp.dot(xb, wx, preferred_element_type=F32) + bx)
    log_a = -LRU_C * r * _softplus(-lam)
    a = jnp.exp(log_a)
    u = jnp.sqrt(jnp.tanh(-log_a) * (1.0 + a * a)) * (i * xc)

    in_group = lax.broadcasted_iota(jnp.int32, (tc, 1), 0) & (SCAN_GROUP - 1)
    shift = 1
    while shift < SCAN_GROUP:
        keep = in_group >= shift
        a_prev = pltpu.roll(a, shift, axis=0)
        u_prev = pltpu.roll(u, shift, axis=0)
        u = jnp.where(keep, a * u_prev + u, u)
        a = jnp.where(keep, a * a_prev, a)
        shift *= 2
    h_last = h_scr[...]
    groups = []
    for g in range(tc // SCAN_GROUP):
        rows = slice(g * SCAN_GROUP, (g + 1) * SCAN_GROUP)
        h_g = a[rows] * h_last + u[rows]
        h_last = h_g[SCAN_GROUP - 1:SCAN_GROUP, :]
        groups.append(h_g)
    h_scr[...] = h_last
    return jnp.concatenate(groups, axis=0) * jax.nn.gelu(gate)


def _attn_kernel(q_ref, k_ref, v_ref, qg_ref, kg_ref, o_ref,
                 qn, kn, m_s, l_s, acc_s, m_d, l_d, acc_d, qp, kp, vp, m_p, l_p, acc_p):
    s_len = q_ref.shape[2]
    blk = ATTN_BLOCK
    lane = lax.broadcasted_iota(jnp.int32, (1, LANES), 1)
    lo = lane < HEAD_DIM
    n_chunk = s_len // NORM_CHUNK
    groups = NORM_CHUNK // PAD_GROUP

    def padded_base(c):
        return pl.multiple_of(c * (groups * PAD_PITCH), 8)

    def to_padded(dst, c, val):
        for g in range(groups):
            dst[pl.ds(padded_base(c) + g * PAD_PITCH, PAD_GROUP), :] = val[g * PAD_GROUP:(g + 1) * PAD_GROUP]

    def from_padded(src, c):
        return jnp.concatenate(
            [src[pl.ds(padded_base(c) + g * PAD_PITCH, PAD_GROUP), :] for g in range(groups)], axis=0)

    same_head = (lax.broadcasted_iota(jnp.int32, (LANES, LANES), 0) // HEAD_DIM
                 == lax.broadcasted_iota(jnp.int32, (LANES, LANES), 1) // HEAD_DIM)
    head_mean = jnp.where(same_head, 1.0 / HEAD_DIM, 0.0).astype(BF16)

    def norm_body(c, carry):
        rows = pl.ds(pl.multiple_of(c * NORM_CHUNK, NORM_CHUNK), NORM_CHUNK)
        for src, g_ref, dst, dst_p, scale in ((q_ref, qg_ref, qn, qp, HEAD_DIM ** -0.5),
                                              (k_ref, kg_ref, kn, kp, 1.0)):
            x = src[0, 0, rows, :]
            x2 = x * x
            x2_hi = x2.astype(BF16)
            x2_lo = (x2 - x2_hi.astype(F32)).astype(BF16)
            ms = (jnp.dot(x2_hi, head_mean, preferred_element_type=F32)
                  + jnp.dot(x2_lo, head_mean, preferred_element_type=F32))
            y = x * lax.rsqrt(ms + EPS) * g_ref[...] * scale
            dst[rows, :] = y
            to_padded(dst_p, c, y)
        to_padded(vp, c, v_ref[0, 0, rows, :])
        return carry

    lax.fori_loop(0, n_chunk, norm_body, 0)

    rows2 = lax.broadcasted_iota(jnp.int32, (HEADS_PER_SLAB * blk, blk), 0)
    qi = rows2 & (blk - 1)
    kj = lax.broadcasted_iota(jnp.int32, (HEADS_PER_SLAB * blk, blk), 1)
    band_prev = kj >= qi
    band_cur = kj <= qi
    nt = (((1,), (1,)), ((), ()))

    def branch(d, q_src, k_src, v_src, stats):
        m_r, l_r, acc_r = stats
        log_d = d.bit_length() - 1
        padded = d == PAD_GROUP
        pitch = PAD_PITCH if padded else d
        span = blk * pitch

        def sl(start):
            return pl.ds(start, blk) if pitch == 1 else pl.ds(start, blk, stride=pitch)

        def body(it, carry):
            n = it >> log_d
            r = it & (d - 1)
            cur = sl(n * span + r)
            prev = sl(jnp.maximum(n - 1, 0) * span + r)
            qt = q_src(cur)
            qs = jnp.concatenate([jnp.where(lo, qt, 0.0), jnp.where(lo, 0.0, qt)], axis=0).astype(BF16)
            s_c = lax.dot_general(qs, k_src(cur).astype(BF16), nt, preferred_element_type=F32)
            s_p = lax.dot_general(qs, k_src(prev).astype(BF16), nt, preferred_element_type=F32)
            s_c = jnp.where(band_cur, s_c, -jnp.inf)
            s_p = jnp.where(band_prev & (n > 0), s_p, -jnp.inf)
            m = jnp.max(jnp.maximum(s_c, s_p), axis=-1, keepdims=True)
            p_c = jnp.exp(s_c - m)
            p_p = jnp.exp(s_p - m)
            l = jnp.sum(p_c + p_p, axis=-1, keepdims=True)
            acc = (jnp.dot(p_c.astype(BF16), v_src(cur).astype(BF16), preferred_element_type=F32)
                   + jnp.dot(p_p.astype(BF16), v_src(prev).astype(BF16), preferred_element_type=F32))
            acc_t = jnp.where(lo, acc[:blk], acc[blk:])
            m_t = jnp.where(lo, jnp.broadcast_to(m[:blk], (blk, LANES)), jnp.broadcast_to(m[blk:], (blk, LANES)))
            l_t = jnp.where(lo, jnp.broadcast_to(l[:blk], (blk, LANES)), jnp.broadcast_to(l[blk:], (blk, LANES)))
            m_r[cur, :] = m_t
            l_r[cur, :] = l_t
            acc_r[cur, :] = acc_t
            return carry

        lax.fori_loop(0, s_len // blk, body, 0, unroll=ATTN_UNROLL)

    token_order = (lambda s: qn[s, :], lambda s: kn[s, :], lambda s: v_ref[0, 0, s, :])
    padded_order = (lambda s: qp[s, :], lambda s: kp[s, :], lambda s: vp[s, :])
    d_near, d_mid, d_far = DILATIONS
    branch(d_near, *token_order, (m_s, l_s, acc_s))
    branch(d_mid, *token_order, (m_d, l_d, acc_d))
    branch(d_far, *padded_order, (m_p, l_p, acc_p))

    def out_body(c, carry):
        rows = pl.ds(pl.multiple_of(c * NORM_CHUNK, NORM_CHUNK), NORM_CHUNK)
        m_a, m_b, m_c = m_s[rows, :], m_d[rows, :], from_padded(m_p, c)
        m_n = jnp.maximum(jnp.maximum(m_a, m_b), m_c)
        c_a = jnp.exp(m_a - m_n)
        c_b = jnp.exp(m_b - m_n)
        c_c = jnp.exp(m_c - m_n)
        o_ref[0, 0, rows, :] = (
            (acc_s[rows, :] * c_a + acc_d[rows, :] * c_b + from_padded(acc_p, c) * c_c)
            / (l_s[rows, :] * c_a + l_d[rows, :] * c_b + from_padded(l_p, c) * c_c))
        return carry

    lax.fori_loop(0, n_chunk, out_body, 0)


def _attention(q, k, v, qg, kg):
    b, n_slab, s, _ = q.shape
    assert s % (ATTN_BLOCK * max(DILATIONS)) == 0 and s % NORM_CHUNK == 0 and max(DILATIONS) == PAD_GROUP
    slab_spec = pl.BlockSpec((1, 1, s, LANES), lambda i, j: (i, j, 0, 0))
    vec = pl.BlockSpec((1, LANES), lambda i, j: (0, 0))
    return pl.pallas_call(
        _attn_kernel,
        out_shape=jax.ShapeDtypeStruct(q.shape, F32),
        grid=(b, n_slab),
        in_specs=[slab_spec, slab_spec, slab_spec, vec, vec],
        out_specs=slab_spec,
        scratch_shapes=([pltpu.VMEM((s, LANES), F32) for _ in range(8)]
                        + [pltpu.VMEM((s // PAD_GROUP * PAD_PITCH, LANES), F32) for _ in range(6)]),
        compiler_params=_cparams("parallel", "parallel"),
        name="dilated_attn",
    )(q, k, v, qg, kg)


def _out_proj_kernel(x_ref, yl_ref, ya_ref, gl_ref, ga_ref, w_ref, gf_ref, wr_ref, br_ref,
                     xn_ref, h2_ref, ids_ref, gw_ref, cnt_ref, carry):
    d_lru = yl_ref.shape[2]
    nl = _rms(yl_ref[0], gl_ref[...]).astype(BF16)
    ya = jnp.concatenate([ya_ref[0, j] for j in range(ya_ref.shape[1])], axis=-1)
    na = _rms(ya, ga_ref[...]).astype(BF16)
    xn = x_ref[0] + (jnp.dot(nl, w_ref[0:d_lru, :], preferred_element_type=F32)
                     + jnp.dot(na, w_ref[d_lru:, :], preferred_element_type=F32))
    xn_ref[0] = xn
    h2 = _rms(xn, gf_ref[...])
    _store_token_tiles(h2_ref, h2)
    hi = h2.astype(BF16)
    lo = (h2 - hi.astype(F32)).astype(BF16)
    both = jnp.dot(hi, wr_ref[...], preferred_element_type=F32)
    logits = (both[:, :ROUTER_LANES] + both[:, ROUTER_LANES:]
              + jnp.dot(lo, wr_ref[:, :ROUTER_LANES], preferred_element_type=F32) + br_ref[...])

    @pl.when((pl.program_id(0) == 0) & (pl.program_id(1) == 0))
    def _():
        carry[...] = jnp.zeros_like(carry)

    ids_ref[0], gw_ref[0] = _route_tile(logits, carry)
    cnt_ref[...] = carry[...]


def _out_proj(x, yl, ya, gl, ga, w, gf, wr, br):
    b, s, d = x.shape
    d_lru = yl.shape[2]
    n_slab = ya.shape[1]
    tm = min(TM_PROJ, s)
    nj = s // tm
    tok = pl.BlockSpec((1, tm, d), lambda i, j: (i, j, 0))
    tiles = pl.BlockSpec((tm * TOKEN_ROWS, LANES), lambda i, j: (i * nj + j, 0))
    lanes = pl.BlockSpec((1, tm, ROUTER_LANES), lambda i, j: (i, j, 0))
    const = lambda shape: pl.BlockSpec(shape, lambda i, j: (0,) * len(shape))
    return pl.pallas_call(
        _out_proj_kernel,
        out_shape=(jax.ShapeDtypeStruct((b, s, d), F32),
                   jax.ShapeDtypeStruct((b * s * TOKEN_ROWS, LANES), jnp.uint32),
                   jax.ShapeDtypeStruct((b, s, ROUTER_LANES), jnp.int32),
                   jax.ShapeDtypeStruct((b, s, ROUTER_LANES), F32),
                   jax.ShapeDtypeStruct((ROUTER_ROWS, LANES), F32)),
        grid=(b, s // tm),
        in_specs=[tok, pl.BlockSpec((1, tm, d_lru), lambda i, j: (i, j, 0)),
                  pl.BlockSpec((1, n_slab, tm, LANES), lambda i, j: (i, 0, j, 0)),
                  const((1, d_lru)), const((1, n_slab * LANES)), const(w.shape), const((1, d)),
                  const(wr.shape), const((1, ROUTER_LANES))],
        out_specs=(tok, tiles, lanes, lanes, const((ROUTER_ROWS, LANES))),
        scratch_shapes=[pltpu.VMEM((ROUTER_ROWS, LANES), F32)],
        compiler_params=_cparams("arbitrary", "arbitrary"),
        name="out_proj",
    )(x, yl, ya, gl, ga, w, gf, wr, br)


def _route_tile(lg, carry):
    tr = lg.shape[0]
    lt = lg.T[:ROUTER_ROWS]
    row = lax.broadcasted_iota(jnp.int32, (ROUTER_ROWS, tr), 0)
    big = jnp.int32(ROUTER_ROWS)

    def argmax(vals):
        top = jnp.max(vals, axis=0, keepdims=True)
        return top, jnp.min(jnp.where(vals == top, row, big), axis=0, keepdims=True)

    g_logit = jnp.where(row < N_GROUPS, lt, -jnp.inf)
    g_top, g_idx = argmax(g_logit)
    p_top = 1.0 / jnp.sum(jnp.exp(g_logit - g_top), axis=0, keepdims=True)
    e_lo = EXPERT_LANE0 + g_idx * EXPERTS_PER_GROUP
    e_logit = jnp.where((row >= e_lo) & (row < e_lo + EXPERTS_PER_GROUP), lt, -jnp.inf)
    v1, i1 = argmax(e_logit)
    v2, i2 = argmax(jnp.where(row == i1, -jnp.inf, e_logit))
    e21 = jnp.exp(v2 - v1)
    w1 = 1.0 / (1.0 + e21) * p_top
    w2 = e21 / (1.0 + e21) * p_top

    hot = ((row == i1) | (row == i2))
    rr = lax.broadcasted_iota(jnp.int32, (tr, tr), 0)
    cc = lax.broadcasted_iota(jnp.int32, (tr, tr), 1)
    earlier = (rr < cc).astype(BF16)
    count = carry[0:ROUTER_ROWS, 0:1]
    prefix = jnp.dot(hot.astype(BF16), earlier, preferred_element_type=F32) + count
    rank1 = jnp.sum(jnp.where(row == i1, prefix, 0.0), axis=0, keepdims=True).astype(jnp.int32)
    rank2 = jnp.sum(jnp.where(row == i2, prefix, 0.0), axis=0, keepdims=True).astype(jnp.int32)
    carry[0:ROUTER_ROWS, :] = carry[0:ROUTER_ROWS, :] + jnp.sum(hot.astype(F32), axis=1, keepdims=True)

    out_row = lax.broadcasted_iota(jnp.int32, (ROUTER_LANES, tr), 0)
    ids_t = jnp.where(out_row == 0, i1 - EXPERT_LANE0,
                      jnp.where(out_row == 1, i2 - EXPERT_LANE0,
                                jnp.where(out_row == 2, rank1, jnp.where(out_row == 3, rank2, 0))))
    gw_t = jnp.where(out_row == 0, w1, jnp.where(out_row == 1, w2, 0.0))
    return ids_t.T, gw_t.T


def _row_copy(src_ref, src_row, dst_ref, dst_row, sem):
    rows = lambda tok: pl.ds(pl.multiple_of(tok * TOKEN_ROWS, TOKEN_ROWS), TOKEN_ROWS)
    return pltpu.make_async_copy(src_ref.at[rows(src_row)], dst_ref.at[rows(dst_row)], sem)


def _dispatch_kernel(dest_ref, h2_ref, xbuf_in_hbm, xbuf_hbm, sem):
    del xbuf_in_hbm
    n_tok = h2_ref.shape[0] // TOKEN_ROWS

    def start(i, carry):
        for k in range(TOP_K):
            _row_copy(h2_ref, i, xbuf_hbm, dest_ref[TOP_K * i + k], sem).start(priority=k)
        return carry

    def wait(i, carry):
        for k in range(TOP_K):
            _row_copy(h2_ref, 0, xbuf_hbm, 0, sem).wait()
        return carry

    lax.fori_loop(0, n_tok, start, 0, unroll=DMA_UNROLL)
    lax.fori_loop(0, n_tok, wait, 0, unroll=DMA_UNROLL)


def _dispatch(dest, h2, xbuf):
    t = h2.shape[0] // TOKEN_ROWS
    td = min(DISPATCH_TILE, t)
    return pl.pallas_call(
        _dispatch_kernel,
        out_shape=jax.ShapeDtypeStruct(xbuf.shape, xbuf.dtype),
        grid=(t // td,),
        in_specs=[pl.BlockSpec((TOP_K * td,), lambda i: (i,), memory_space=pltpu.SMEM),
                  pl.BlockSpec((td * TOKEN_ROWS, LANES), lambda i: (i, 0)),
                  pl.BlockSpec(memory_space=pl.ANY)],
        out_specs=pl.BlockSpec(memory_space=pl.ANY),
        scratch_shapes=[pltpu.SemaphoreType.DMA],
        input_output_aliases={2: 0},
        compiler_params=_cparams("arbitrary"),
        name="dispatch",
    )(dest, h2, xbuf)


def _expert_kernel(be_ref, used_ref, x_ref, wg_ref, wu_ref, wd_ref, y_ref, wg_s, wu_s, wd_s):
    i = pl.program_id(0)
    new_expert = (i == 0) | (be_ref[i] != be_ref[jnp.maximum(i - 1, 0)])
    in_use = i < used_ref[0]

    @pl.when(new_expert & in_use)
    def _():
        wg_s[...] = wg_ref[0, 0].astype(BF16)
        wu_s[...] = wu_ref[0, 0].astype(BF16)
        wd_s[...] = wd_ref[0, 0].astype(BF16)

    @pl.when(in_use)
    def _():
        x = _load_token_tiles(x_ref, MOE_BLOCK).astype(BF16)
        g = jnp.dot(x, wg_s[...], preferred_element_type=F32)
        u = jnp.dot(x, wu_s[...], preferred_element_type=F32)
        hid = (jax.nn.silu(g) * u).astype(BF16)
        _store_token_tiles(y_ref, jnp.dot(hid, wd_s[...], preferred_element_type=F32))

    @pl.when(jnp.logical_not(in_use))
    def _():
        y_ref[...] = jnp.zeros_like(y_ref)


def _experts(blk_expert, blocks_used, xbuf, wg, wu, wd, layer):
    d, de = wg.shape[2], wg.shape[3]
    n_blk = xbuf.shape[0] // (MOE_BLOCK * TOKEN_ROWS)
    rows = pl.BlockSpec((MOE_BLOCK * TOKEN_ROWS, LANES), lambda i, be, used: (i, 0))
    w_in = pl.BlockSpec((1, 1, d, de), lambda i, be, used: (layer, be[i], 0, 0))
    w_out = pl.BlockSpec((1, 1, de, d), lambda i, be, used: (layer, be[i], 0, 0))
    return pl.pallas_call(
        _expert_kernel,
        out_shape=jax.ShapeDtypeStruct(xbuf.shape, xbuf.dtype),
        grid_spec=pltpu.PrefetchScalarGridSpec(
            num_scalar_prefetch=2,
            grid=(n_blk,),
            in_specs=[rows, w_in, w_in, w_out],
            out_specs=rows,
            scratch_shapes=[pltpu.VMEM((d, de), BF16), pltpu.VMEM((d, de), BF16), pltpu.VMEM((de, d), BF16)]),
        compiler_params=_cparams("arbitrary"),
        name="experts",
    )(blk_expert, blocks_used, xbuf, wg, wu, wd)


def _combine_kernel(dest_ref, dest_next_ref, gw_ref, x_ref, ybuf_hbm, o_ref, rows0, rows1, sems):
    n_tok = x_ref.shape[0]
    step = pl.program_id(0)
    slot = step & 1

    def issue(idx_ref, into):
        def start(i, carry):
            for k, rows in enumerate((rows0, rows1)):
                _row_copy(ybuf_hbm, idx_ref[TOP_K * i + k], rows.at[into], i, sems.at[into]).start(priority=k)
            return carry
        lax.fori_loop(0, n_tok, start, 0, unroll=DMA_UNROLL)

    @pl.when(step == 0)
    def _():
        issue(dest_ref, 0)

    @pl.when(step + 1 < pl.num_programs(0))
    def _():
        issue(dest_next_ref, 1 - slot)

    def wait(i, carry):
        for rows in (rows0, rows1):
            _row_copy(ybuf_hbm, 0, rows.at[slot], 0, sems.at[slot]).wait()
        return carry

    lax.fori_loop(0, n_tok, wait, 0, unroll=DMA_UNROLL)
    gw = gw_ref[...]
    o_ref[...] = x_ref[...] + (gw[:, 0:1] * _load_token_tiles(rows0.at[slot], n_tok)
                               + gw[:, 1:2] * _load_token_tiles(rows1.at[slot], n_tok))


def _combine(dest, gw, x, ybuf):
    t, d = x.shape
    tc = min(DMA_TILE, t)
    steps = t // tc
    tok = pl.BlockSpec((tc, d), lambda i: (i, 0))
    slots = pltpu.VMEM((2, tc * TOKEN_ROWS, LANES), ybuf.dtype)
    return pl.pallas_call(
        _combine_kernel,
        out_shape=jax.ShapeDtypeStruct((t, d), F32),
        grid=(steps,),
        in_specs=[pl.BlockSpec((TOP_K * tc,), lambda i: (i,), memory_space=pltpu.SMEM),
                  pl.BlockSpec((TOP_K * tc,), lambda i: (jnp.minimum(i + 1, steps - 1),), memory_space=pltpu.SMEM),
                  pl.BlockSpec((tc, ROUTER_LANES), lambda i: (i, 0)), tok,
                  pl.BlockSpec(memory_space=pl.ANY)],
        out_specs=tok,
        scratch_shapes=[slots, slots, pltpu.SemaphoreType.DMA((2,))],
        compiler_params=_cparams("arbitrary"),
        name="combine",
    )(dest, dest, gw, x, ybuf)


def _block_diag(w):
    nb, bd, _ = w.shape
    per = LANES // bd
    w = w.reshape(nb // per, per, bd, bd)
    eye = jnp.eye(per, dtype=w.dtype)
    return (eye[None, :, None, :, None] * w[:, :, :, None, :]).reshape(nb // per, LANES, LANES)


def _moe(x, h2, ids, gw, cnt, xbuf, wg, wu, wd, layer):
    t, d = x.shape
    counts = cnt[EXPERT_LANE0:EXPERT_LANE0 + N_EXPERTS, 0].astype(jnp.int32)
    padded = (counts + MOE_BLOCK - 1) // MOE_BLOCK * MOE_BLOCK
    pend = jnp.cumsum(padded)
    pstart = pend - padded
    lookup = lambda table, idx: jnp.sum(
        jnp.where(idx[..., None] == jnp.arange(table.shape[0], dtype=jnp.int32), table, 0), axis=-1)
    dest = (lookup(pstart, ids[:, 0:TOP_K]) + ids[:, TOP_K:2 * TOP_K]).reshape(t * TOP_K)
    n_blk = xbuf.shape[0] // (MOE_BLOCK * TOKEN_ROWS)
    blk_start = jnp.arange(n_blk, dtype=jnp.int32) * MOE_BLOCK
    blk_expert = jnp.minimum(jnp.sum(pend[None, :] <= blk_start[:, None], axis=1), N_EXPERTS - 1).astype(jnp.int32)
    blocks_used = (pend[-1:] // MOE_BLOCK).astype(jnp.int32)
    xbuf = _dispatch(dest, h2, xbuf)
    ybuf = _experts(blk_expert, blocks_used, xbuf, wg, wu, wd, layer)
    return _combine(dest, gw, x, ybuf), xbuf


def kernel(x, norm_mix, w_in, conv_w, conv_b, lru_w_a, lru_b_a, lru_w_x, lru_b_x, lru_lambda, q_norm, k_norm, norm_out_lru, norm_out_attn, w_out, norm_ffn, router_group_w, router_group_b, router_expert_w, router_expert_b, w_gate, w_up, w_down):
    b, s, d = x.shape
    depth = w_in.shape[0]
    d_attn = norm_out_attn.shape[1]
    pad = ROUTER_LANES - N_GROUPS - N_EXPERTS
    row = lambda v: v.reshape(1, -1)
    t = b * s
    xbuf = jnp.zeros(((t * TOP_K + N_EXPERTS * MOE_BLOCK) * TOKEN_ROWS, LANES), jnp.uint32)
    for l in range(depth):
        y_lru, q, k, v = _in_proj(x, row(norm_mix[l]), w_in[l].astype(BF16), conv_w[l], row(conv_b[l]),
                                  _block_diag(lru_w_a[l]).astype(BF16), row(lru_b_a[l]),
                                  _block_diag(lru_w_x[l]).astype(BF16), row(lru_b_x[l]),
                                  row(lru_lambda[l]), d_attn)
        y_attn = _attention(q, k, v, row(jnp.tile(q_norm[l], HEADS_PER_SLAB)),
                            row(jnp.tile(k_norm[l], HEADS_PER_SLAB)))
        wr = jnp.pad(jnp.concatenate([router_group_w[l], router_expert_w[l]], axis=1), ((0, 0), (0, pad)))
        wr_hi = wr.astype(BF16)
        wr = jnp.concatenate([wr_hi, (wr - wr_hi.astype(F32)).astype(BF16)], axis=1)
        br = jnp.pad(jnp.concatenate([router_group_b[l], router_expert_b[l]]), (0, pad))
        xn, h2, ids, gw, cnt = _out_proj(x, y_lru, y_attn, row(norm_out_lru[l]), row(norm_out_attn[l]),
                                         w_out[l].astype(BF16), row(norm_ffn[l]), wr, row(br))
        x, xbuf = _moe(xn.reshape(t, d), h2, ids.reshape(t, ROUTER_LANES), gw.reshape(t, ROUTER_LANES), cnt,
                       xbuf, w_gate, w_up, w_down, l)
        x = x.reshape(b, s, d)
    return x
```

```python
import functools

import jax
import jax.numpy as jnp
from jax import lax
from jax.experimental import pallas as pl
from jax.experimental.pallas import tpu as pltpu

F32 = jnp.float32
BF16 = jnp.bfloat16

EPS = 1e-6
LANES = 128
HEAD_DIM = 64
HEADS_PER_SLAB = LANES // HEAD_DIM
ATTN_BLOCK = 128
DILATIONS = (1, 4, 16)
CONV_WIDTH = 4
LRU_C = 8.0
N_GROUPS = 4
EXPERTS_PER_GROUP = 8
N_EXPERTS = N_GROUPS * EXPERTS_PER_GROUP
TOP_K = 2
ROUTER_LANES = LANES
EXPERT_LANE0 = N_GROUPS
ROUTER_ROWS = 40
VMEM_LIMIT = 56 * 1024 * 1024

TM_PROJ = 512
SCAN_GROUP = 8
PAD_GROUP = 16
PAD_PITCH = 24
ATTN_UNROLL = 8
NORM_CHUNK = 512
MOE_BLOCK = 1024
TOKEN_ROWS = 4
DMA_TILE = 1024
DISPATCH_TILE = 1024
DMA_UNROLL = 8


def _rms(x, g):
    return x * lax.rsqrt(jnp.mean(x * x, axis=-1, keepdims=True) + EPS) * g


def _log1p(z):
    w = 1.0 + z
    return jnp.where(w == 1.0, z, z * jnp.log(w) / jnp.where(w == 1.0, 1.0, w - 1.0))


def _softplus(x):
    return jnp.maximum(x, 0.0) + _log1p(jnp.exp(-jnp.abs(x)))


def _store_token_tiles(ref, val):
    n, d = val.shape
    assert d == 2 * TOKEN_ROWS * LANES
    for r in range(TOKEN_ROWS):
        hi = val[:, r * LANES:(r + 1) * LANES]
        lo = val[:, d // 2 + r * LANES:d // 2 + (r + 1) * LANES]
        hi = lax.bitcast_convert_type(hi.astype(BF16).astype(F32), jnp.uint32)
        lo = lax.bitcast_convert_type(lo.astype(BF16).astype(F32), jnp.uint32)
        ref[pl.ds(r, n, stride=TOKEN_ROWS), :] = hi | (lo >> 16)


def _load_token_tiles(ref, n):
    words = [ref[pl.ds(r, n, stride=TOKEN_ROWS), :] for r in range(TOKEN_ROWS)]
    hi = [lax.bitcast_convert_type(w & jnp.uint32(0xFFFF0000), F32) for w in words]
    lo = [lax.bitcast_convert_type(w << 16, F32) for w in words]
    return jnp.concatenate(hi + lo, axis=-1)


def _cparams(*sem):
    return pltpu.CompilerParams(dimension_semantics=sem, vmem_limit_bytes=VMEM_LIMIT)


def _in_proj_kernel(x_ref, g_ref, w_ref, cw_ref, cb_ref, wa_ref, ba_ref, wx_ref, bx_ref, lam_ref,
                    y_ref, q_ref, k_ref, v_ref, tail_scr, h_scr, *, d_lru, d_attn):
    h = _rms(x_ref[0], g_ref[...]).astype(BF16)

    def cols(lo, n):
        return jnp.dot(h, w_ref[:, lo:lo + n], preferred_element_type=F32)

    @pl.when(pl.program_id(1) == 0)
    def _():
        tail_scr[...] = jnp.zeros_like(tail_scr)
        h_scr[...] = jnp.zeros_like(h_scr)

    xl, gate = cols(0, d_lru), cols(d_lru, d_lru)
    for grp in range(d_lru // LANES):
        sl = slice(grp * LANES, (grp + 1) * LANES)
        y_ref[0, :, sl] = _rg_lru_tile(
            xl[:, sl], gate[:, sl], cw_ref[:, sl], cb_ref[:, sl], wa_ref[grp], ba_ref[:, sl],
            wx_ref[grp], bx_ref[:, sl], lam_ref[:, sl], tail_scr.at[grp], h_scr.at[grp])
    base = 2 * d_lru
    for ref in (q_ref, k_ref, v_ref):
        heads = cols(base, d_attn)
        for j in range(d_attn // LANES):
            ref[0, j] = heads[:, j * LANES:(j + 1) * LANES]
        base += d_attn


def _in_proj(x, g, w, cw, cb, wa, ba, wx, bx, lam, d_attn):
    b, s, d = x.shape
    d_lru = cw.shape[1]
    tm = min(TM_PROJ, s)
    n_slab = d_attn // LANES
    slab = jax.ShapeDtypeStruct((b, n_slab, s, LANES), F32)
    slab_spec = pl.BlockSpec((1, n_slab, tm, LANES), lambda i, j: (i, 0, j, 0))
    const = lambda shape: pl.BlockSpec(shape, lambda i, j: (0,) * len(shape))
    vec = const((1, d_lru))
    return pl.pallas_call(
        functools.partial(_in_proj_kernel, d_lru=d_lru, d_attn=d_attn),
        out_shape=(jax.ShapeDtypeStruct((b, s, d_lru), F32), slab, slab, slab),
        grid=(b, s // tm),
        in_specs=[pl.BlockSpec((1, tm, d), lambda i, j: (i, j, 0)), const((1, d)), const(w.shape),
                  const((CONV_WIDTH, d_lru)), vec, const(wa.shape), vec, const(wx.shape), vec, vec],
        out_specs=(pl.BlockSpec((1, tm, d_lru), lambda i, j: (i, j, 0)), slab_spec, slab_spec, slab_spec),
        scratch_shapes=[pltpu.VMEM((d_lru // LANES, 8, LANES), F32), pltpu.VMEM((d_lru // LANES, 1, LANES), F32)],
        compiler_params=_cparams("parallel", "arbitrary"),
        name="in_proj",
    )(x, g, w, cw, cb, wa, ba, wx, bx, lam)


def _rg_lru_tile(x, gate, cw, cb, wa, ba, wx, bx, lam, tail_scr, h_scr):
    tc = x.shape[0]
    halo = tail_scr.shape[0]
    tail = tail_scr[...]
    first = lax.broadcasted_iota(jnp.int32, (halo, 1), 0)
    xc = cb
    for j in range(CONV_WIDTH):
        back = CONV_WIDTH - 1 - j
        if back == 0:
            xs = x
        else:
            shifted = pltpu.roll(x, back, axis=0)
            head = jnp.where(first < back, pltpu.roll(tail, back, axis=0), shifted[0:halo])
            xs = jnp.concatenate([head, shifted[halo:]], axis=0)
        xc = xc + cw[j:j + 1, :] * xs
    tail_scr[...] = x[tc - halo:tc, :]

    xb = xc.astype(BF16)
    r = jax.nn.sigmoid(jnp.dot(xb, wa, preferred_element_type=F32) + ba)
    i = jax.nn.sigmoid(jnp.dot(xb, wx, preferred_element_type=F32) + bx)
    log_a = -LRU_C * r * _softplus(-lam)
    a = jnp.exp(log_a)
    u = jnp.sqrt(jnp.tanh(-log_a) * (1.0 + a * a)) * (i * xc)

    in_group = lax.broadcasted_iota(jnp.int32, (tc, 1), 0) & (SCAN_GROUP - 1)
    shift = 1
    while shift < SCAN_GROUP:
        keep = in_group >= shift
        a_prev = pltpu.roll(a, shift, axis=0)
        u_prev = pltpu.roll(u, shift, axis=0)
        u = jnp.where(keep, a * u_prev + u, u)
        a = jnp.where(keep, a * a_prev, a)
        shift *= 2
    h_last = h_scr[...]
    groups = []
    for g in range(tc // SCAN_GROUP):
        rows = slice(g * SCAN_GROUP, (g + 1) * SCAN_GROUP)
        h_g = a[rows] * h_last + u[rows]
        h_last = h_g[SCAN_GROUP - 1:SCAN_GROUP, :]
        groups.append(h_g)
    h_scr[...] = h_last
    return jnp.concatenate(groups, axis=0) * jax.nn.gelu(gate)


def _attn_kernel(q_ref, k_ref, v_ref, qg_ref, kg_ref, o_ref,
                 qn, kn, m_s, l_s, acc_s, m_d, l_d, acc_d, qp, kp, vp, m_p, l_p, acc_p):
    s_len = q_ref.shape[2]
    blk = ATTN_BLOCK
    lane = lax.broadcasted_iota(jnp.int32, (1, LANES), 1)
    lo = lane < HEAD_DIM
    n_chunk = s_len // NORM_CHUNK
    groups = NORM_CHUNK // PAD_GROUP

    def padded_base(c):
        return pl.multiple_of(c * (groups * PAD_PITCH), 8)

    def to_padded(dst, c, val):
        for g in range(groups):
            dst[pl.ds(padded_base(c) + g * PAD_PITCH, PAD_GROUP), :] = val[g * PAD_GROUP:(g + 1) * PAD_GROUP]

    def from_padded(src, c):
        return jnp.concatenate(
            [src[pl.ds(padded_base(c) + g * PAD_PITCH, PAD_GROUP), :] for g in range(groups)], axis=0)

    same_head = (lax.broadcasted_iota(jnp.int32, (LANES, LANES), 0) // HEAD_DIM
                 == lax.broadcasted_iota(jnp.int32, (LANES, LANES), 1) // HEAD_DIM)
    head_mean = jnp.where(same_head, 1.0 / HEAD_DIM, 0.0).astype(BF16)

    def norm_body(c, carry):
        rows = pl.ds(pl.multiple_of(c * NORM_CHUNK, NORM_CHUNK), NORM_CHUNK)
        for src, g_ref, dst, dst_p, scale in ((q_ref, qg_ref, qn, qp, HEAD_DIM ** -0.5),
                                              (k_ref, kg_ref, kn, kp, 1.0)):
            x = src[0, 0, rows, :]
            x2 = x * x
            x2_hi = x2.astype(BF16)
            x2_lo = (x2 - x2_hi.astype(F32)).astype(BF16)
            ms = (jnp.dot(x2_hi, head_mean, preferred_element_type=F32)
                  + jnp.dot(x2_lo, head_mean, preferred_element_type=F32))
            y = x * lax.rsqrt(ms + EPS) * g_ref[...] * scale
            dst[rows, :] = y
            to_padded(dst_p, c, y)
        to_padded(vp, c, v_ref[0, 0, rows, :])
        return carry

    lax.fori_loop(0, n_chunk, norm_body, 0)

    rows2 = lax.broadcasted_iota(jnp.int32, (HEADS_PER_SLAB * blk, blk), 0)
    qi = rows2 & (blk - 1)
    kj = lax.broadcasted_iota(jnp.int32, (HEADS_PER_SLAB * blk, blk), 1)
    band_prev = kj >= qi
    band_cur = kj <= qi
    nt = (((1,), (1,)), ((), ()))

    def branch(d, q_src, k_src, v_src, stats):
        m_r, l_r, acc_r = stats
        log_d = d.bit_length() - 1
        padded = d == PAD_GROUP
        pitch = PAD_PITCH if padded else d
        span = blk * pitch

        def sl(start):
            return pl.ds(start, blk) if pitch == 1 else pl.ds(start, blk, stride=pitch)

        def body(it, carry):
            n = it >> log_d
            r = it & (d - 1)
            cur = sl(n * span + r)
            prev = sl(jnp.maximum(n - 1, 0) * span + r)
            qt = q_src(cur)
            qs = jnp.concatenate([jnp.where(lo, qt, 0.0), jnp.where(lo, 0.0, qt)], axis=0).astype(BF16)
            s_c = lax.dot_general(qs, k_src(cur).astype(BF16), nt, preferred_element_type=F32)
            s_p = lax.dot_general(qs, k_src(prev).astype(BF16), nt, preferred_element_type=F32)
            s_c = jnp.where(band_cur, s_c, -jnp.inf)
            s_p = jnp.where(band_prev & (n > 0), s_p, -jnp.inf)
            m = jnp.max(jnp.maximum(s_c, s_p), axis=-1, keepdims=True)
            p_c = jnp.exp(s_c - m)
            p_p = jnp.exp(s_p - m)
            l = jnp.sum(p_c + p_p, axis=-1, keepdims=True)
            acc = (jnp.dot(p_c.astype(BF16), v_src(cur).astype(BF16), preferred_element_type=F32)
                   + jnp.dot(p_p.astype(BF16), v_src(prev).astype(BF16), preferred_element_type=F32))
            acc_t = jnp.where(lo, acc[:blk], acc[blk:])
            m_t = jnp.where(lo, jnp.broadcast_to(m[:blk], (blk, LANES)), jnp.broadcast_to(m[blk:], (blk, LANES)))
            l_t = jnp.where(lo, jnp.broadcast_to(l[:blk], (blk, LANES)), jnp.broadcast_to(l[blk:], (blk, LANES)))
            m_r[cur, :] = m_t
            l_r[cur, :] = l_t
            acc_r[cur, :] = acc_t
            return carry

        lax.fori_loop(0, s_len // blk, body, 0, unroll=ATTN_UNROLL)

    token_order = (lambda s: qn[s, :], lambda s: kn[s, :], lambda s: v_ref[0, 0, s, :])
    padded_order = (lambda s: qp[s, :], lambda s: kp[s, :], lambda s: vp[s, :])
    d_near, d_mid, d_far = DILATIONS
    branch(d_near, *token_order, (m_s, l_s, acc_s))
    branch(d_mid, *token_order, (m_d, l_d, acc_d))
    branch(d_far, *padded_order, (m_p, l_p, acc_p))

    def out_body(c, carry):
        rows = pl.ds(pl.multiple_of(c * NORM_CHUNK, NORM_CHUNK), NORM_CHUNK)
        m_a, m_b, m_c = m_s[rows, :], m_d[rows, :], from_padded(m_p, c)
        m_n = jnp.maximum(jnp.maximum(m_a, m_b), m_c)
        c_a = jnp.exp(m_a - m_n)
        c_b = jnp.exp(m_b - m_n)
        c_c = jnp.exp(m_c - m_n)
        o_ref[0, 0, rows, :] = (
            (acc_s[rows, :] * c_a + acc_d[rows, :] * c_b + from_padded(acc_p, c) * c_c)
            / (l_s[rows, :] * c_a + l_d[rows, :] * c_b + from_padded(l_p, c) * c_c))
        return carry

    lax.fori_loop(0, n_chunk, out_body, 0)


def _attention(q, k, v, qg, kg):
    b, n_slab, s, _ = q.shape
    assert s % (ATTN_BLOCK * max(DILATIONS)) == 0 and s % NORM_CHUNK == 0 and max(DILATIONS) == PAD_GROUP
    slab_spec = pl.BlockSpec((1, 1, s, LANES), lambda i, j: (i, j, 0, 0))
    vec = pl.BlockSpec((1, LANES), lambda i, j: (0, 0))
    return pl.pallas_call(
        _attn_kernel,
        out_shape=jax.ShapeDtypeStruct(q.shape, F32),
        grid=(b, n_slab),
        in_specs=[slab_spec, slab_spec, slab_spec, vec, vec],
        out_specs=slab_spec,
        scratch_shapes=([pltpu.VMEM((s, LANES), F32) for _ in range(8)]
                        + [pltpu.VMEM((s // PAD_GROUP * PAD_PITCH, LANES), F32) for _ in range(6)]),
        compiler_params=_cparams("parallel", "parallel"),
        name="dilated_attn",
    )(q, k, v, qg, kg)


def _out_proj_kernel(x_ref, yl_ref, ya_ref, gl_ref, ga_ref, w_ref, gf_ref, wr_ref, br_ref,
                     xn_ref, h2_ref, ids_ref, gw_ref, cnt_ref, carry):
    d_lru = yl_ref.shape[2]
    nl = _rms(yl_ref[0], gl_ref[...]).astype(BF16)
    ya = jnp.concatenate([ya_ref[0, j] for j in range(ya_ref.shape[1])], axis=-1)
    na = _rms(ya, ga_ref[...]).astype(BF16)
    xn = x_ref[0] + (jnp.dot(nl, w_ref[0:d_lru, :], preferred_element_type=F32)
                     + jnp.dot(na, w_ref[d_lru:, :], preferred_element_type=F32))
    xn_ref[0] = xn
    h2 = _rms(xn, gf_ref[...])
    _store_token_tiles(h2_ref, h2)
    hi = h2.astype(BF16)
    lo = (h2 - hi.astype(F32)).astype(BF16)
    both = jnp.dot(hi, wr_ref[...], preferred_element_type=F32)
    logits = (both[:, :ROUTER_LANES] + both[:, ROUTER_LANES:]
              + jnp.dot(lo, wr_ref[:, :ROUTER_LANES], preferred_element_type=F32) + br_ref[...])

    @pl.when((pl.program_id(0) == 0) & (pl.program_id(1) == 0))
    def _():
        carry[...] = jnp.zeros_like(carry)

    ids_ref[0], gw_ref[0] = _route_tile(logits, carry)
    cnt_ref[...] = carry[...]


def _out_proj(x, yl, ya, gl, ga, w, gf, wr, br):
    b, s, d = x.shape
    d_lru = yl.shape[2]
    n_slab = ya.shape[1]
    tm = min(TM_PROJ, s)
    nj = s // tm
    tok = pl.BlockSpec((1, tm, d), lambda i, j: (i, j, 0))
    tiles = pl.BlockSpec((tm * TOKEN_ROWS, LANES), lambda i, j: (i * nj + j, 0))
    lanes = pl.BlockSpec((1, tm, ROUTER_LANES), lambda i, j: (i, j, 0))
    const = lambda shape: pl.BlockSpec(shape, lambda i, j: (0,) * len(shape))
    return pl.pallas_call(
        _out_proj_kernel,
        out_shape=(jax.ShapeDtypeStruct((b, s, d), F32),
                   jax.ShapeDtypeStruct((b * s * TOKEN_ROWS, LANES), jnp.uint32),
                   jax.ShapeDtypeStruct((b, s, ROUTER_LANES), jnp.int32),
                   jax.ShapeDtypeStruct((b, s, ROUTER_LANES), F32),
                   jax.ShapeDtypeStruct((ROUTER_ROWS, LANES), F32)),
        grid=(b, s // tm),
        in_specs=[tok, pl.BlockSpec((1, tm, d_lru), lambda i, j: (i, j, 0)),
                  pl.BlockSpec((1, n_slab, tm, LANES), lambda i, j: (i, 0, j, 0)),
                  const((1, d_lru)), const((1, n_slab * LANES)), const(w.shape), const((1, d)),
                  const(wr.shape), const((1, ROUTER_LANES))],
        out_specs=(tok, tiles, lanes, lanes, const((ROUTER_ROWS, LANES))),
        scratch_shapes=[pltpu.VMEM((ROUTER_ROWS, LANES), F32)],
        compiler_params=_cparams("arbitrary", "arbitrary"),
        name="out_proj",
    )(x, yl, ya, gl, ga, w, gf, wr, br)


def _route_tile(lg, carry):
    tr = lg.shape[0]
    lt = lg.T[:ROUTER_ROWS]
    row = lax.broadcasted_iota(jnp.int32, (ROUTER_ROWS, tr), 0)
    big = jnp.int32(ROUTER_ROWS)

    def argmax(vals):
        top = jnp.max(vals, axis=0, keepdims=True)
        return top, jnp.min(jnp.where(vals == top, row, big), axis=0, keepdims=True)

    g_logit = jnp.where(row < N_GROUPS, lt, -jnp.inf)
    g_top, g_idx = argmax(g_logit)
    p_top = 1.0 / jnp.sum(jnp.exp(g_logit - g_top), axis=0, keepdims=True)
    e_lo = EXPERT_LANE0 + g_idx * EXPERTS_PER_GROUP
    e_logit = jnp.where((row >= e_lo) & (row < e_lo + EXPERTS_PER_GROUP), lt, -jnp.inf)
    v1, i1 = argmax(e_logit)
    v2, i2 = argmax(jnp.where(row == i1, -jnp.inf, e_logit))
    e21 = jnp.exp(v2 - v1)
    w1 = 1.0 / (1.0 + e21) * p_top
    w2 = e21 / (1.0 + e21) * p_top

    hot = ((row == i1) | (row == i2))
    rr = lax.broadcasted_iota(jnp.int32, (tr, tr), 0)
    cc = lax.broadcasted_iota(jnp.int32, (tr, tr), 1)
    earlier = (rr < cc).astype(BF16)
    count = carry[0:ROUTER_ROWS, 0:1]
    prefix = jnp.dot(hot.astype(BF16), earlier, preferred_element_type=F32) + count
    rank1 = jnp.sum(jnp.where(row == i1, prefix, 0.0), axis=0, keepdims=True).astype(jnp.int32)
    rank2 = jnp.sum(jnp.where(row == i2, prefix, 0.0), axis=0, keepdims=True).astype(jnp.int32)
    carry[0:ROUTER_ROWS, :] = carry[0:ROUTER_ROWS, :] + jnp.sum(hot.astype(F32), axis=1, keepdims=True)

    out_row = lax.broadcasted_iota(jnp.int32, (ROUTER_LANES, tr), 0)
    ids_t = jnp.where(out_row == 0, i1 - EXPERT_LANE0,
                      jnp.where(out_row == 1, i2 - EXPERT_LANE0,
                                jnp.where(out_row == 2, rank1, jnp.where(out_row == 3, rank2, 0))))
    gw_t = jnp.where(out_row == 0, w1, jnp.where(out_row == 1, w2, 0.0))
    return ids_t.T, gw_t.T


def _row_copy(src_ref, src_row, dst_ref, dst_row, sem):
    rows = lambda tok: pl.ds(pl.multiple_of(tok * TOKEN_ROWS, TOKEN_ROWS), TOKEN_ROWS)
    return pltpu.make_async_copy(src_ref.at[rows(src_row)], dst_ref.at[rows(dst_row)], sem)


def _dispatch_kernel(dest_ref, h2_ref, xbuf_in_hbm, xbuf_hbm, sem):
    del xbuf_in_hbm
    n_tok = h2_ref.shape[0] // TOKEN_ROWS

    def start(i, carry):
        for k in range(TOP_K):
            _row_copy(h2_ref, i, xbuf_hbm, dest_ref[TOP_K * i + k], sem).start(priority=k)
        return carry

    def wait(i, carry):
        for k in range(TOP_K):
            _row_copy(h2_ref, 0, xbuf_hbm, 0, sem).wait()
        return carry

    lax.fori_loop(0, n_tok, start, 0, unroll=DMA_UNROLL)
    lax.fori_loop(0, n_tok, wait, 0, unroll=DMA_UNROLL)


def _dispatch(dest, h2, xbuf):
    t = h2.shape[0] // TOKEN_ROWS
    td = min(DISPATCH_TILE, t)
    return pl.pallas_call(
        _dispatch_kernel,
        out_shape=jax.ShapeDtypeStruct(xbuf.shape, xbuf.dtype),
        grid=(t // td,),
        in_specs=[pl.BlockSpec((TOP_K * td,), lambda i: (i,), memory_space=pltpu.SMEM),
                  pl.BlockSpec((td * TOKEN_ROWS, LANES), lambda i: (i, 0)),
                  pl.BlockSpec(memory_space=pl.ANY)],
        out_specs=pl.BlockSpec(memory_space=pl.ANY),
        scratch_shapes=[pltpu.SemaphoreType.DMA],
        input_output_aliases={2: 0},
        compiler_params=_cparams("arbitrary"),
        name="dispatch",
    )(dest, h2, xbuf)


def _expert_kernel(be_ref, used_ref, x_ref, wg_ref, wu_ref, wd_ref, y_ref, wg_s, wu_s, wd_s):
    i = pl.program_id(0)
    new_expert = (i == 0) | (be_ref[i] != be_ref[jnp.maximum(i - 1, 0)])
    in_use = i < used_ref[0]

    @pl.when(new_expert & in_use)
    def _():
        wg_s[...] = wg_ref[0, 0].astype(BF16)
        wu_s[...] = wu_ref[0, 0].astype(BF16)
        wd_s[...] = wd_ref[0, 0].astype(BF16)

    @pl.when(in_use)
    def _():
        x = _load_token_tiles(x_ref, MOE_BLOCK).astype(BF16)
        g = jnp.dot(x, wg_s[...], preferred_element_type=F32)
        u = jnp.dot(x, wu_s[...], preferred_element_type=F32)
        hid = (jax.nn.silu(g) * u).astype(BF16)
        _store_token_tiles(y_ref, jnp.dot(hid, wd_s[...], preferred_element_type=F32))

    @pl.when(jnp.logical_not(in_use))
    def _():
        y_ref[...] = jnp.zeros_like(y_ref)


def _experts(blk_expert, blocks_used, xbuf, wg, wu, wd, layer):
    d, de = wg.shape[2], wg.shape[3]
    n_blk = xbuf.shape[0] // (MOE_BLOCK * TOKEN_ROWS)
    rows = pl.BlockSpec((MOE_BLOCK * TOKEN_ROWS, LANES), lambda i, be, used: (i, 0))
    w_in = pl.BlockSpec((1, 1, d, de), lambda i, be, used: (layer, be[i], 0, 0))
    w_out = pl.BlockSpec((1, 1, de, d), lambda i, be, used: (layer, be[i], 0, 0))
    return pl.pallas_call(
        _expert_kernel,
        out_shape=jax.ShapeDtypeStruct(xbuf.shape, xbuf.dtype),
        grid_spec=pltpu.PrefetchScalarGridSpec(
            num_scalar_prefetch=2,
            grid=(n_blk,),
            in_specs=[rows, w_in, w_in, w_out],
            out_specs=rows,
            scratch_shapes=[pltpu.VMEM((d, de), BF16), pltpu.VMEM((d, de), BF16), pltpu.VMEM((de, d), BF16)]),
        compiler_params=_cparams("arbitrary"),
        name="experts",
    )(blk_expert, blocks_used, xbuf, wg, wu, wd)


def _combine_kernel(dest_ref, dest_next_ref, gw_ref, x_ref, ybuf_hbm, o_ref, rows0, rows1, sems):
    n_tok = x_ref.shape[0]
    step = pl.program_id(0)
    slot = step & 1

    def issue(idx_ref, into):
        def start(i, carry):
            for k, rows in enumerate((rows0, rows1)):
                _row_copy(ybuf_hbm, idx_ref[TOP_K * i + k], rows.at[into], i, sems.at[into]).start(priority=k)
            return carry
        lax.fori_loop(0, n_tok, start, 0, unroll=DMA_UNROLL)

    @pl.when(step == 0)
    def _():
        issue(dest_ref, 0)

    @pl.when(step + 1 < pl.num_programs(0))
    def _():
        issue(dest_next_ref, 1 - slot)

    def wait(i, carry):
        for rows in (rows0, rows1):
            _row_copy(ybuf_hbm, 0, rows.at[slot], 0, sems.at[slot]).wait()
        return carry

    lax.fori_loop(0, n_tok, wait, 0, unroll=DMA_UNROLL)
    gw = gw_ref[...]
    o_ref[...] = x_ref[...] + (gw[:, 0:1] * _load_token_tiles(rows0.at[slot], n_tok)
                               + gw[:, 1:2] * _load_token_tiles(rows1.at[slot], n_tok))


def _combine(dest, gw, x, ybuf):
    t, d = x.shape
    tc = min(DMA_TILE, t)
    steps = t // tc
    tok = pl.BlockSpec((tc, d), lambda i: (i, 0))
    slots = pltpu.VMEM((2, tc * TOKEN_ROWS, LANES), ybuf.dtype)
    return pl.pallas_call(
        _combine_kernel,
        out_shape=jax.ShapeDtypeStruct((t, d), F32),
        grid=(steps,),
        in_specs=[pl.BlockSpec((TOP_K * tc,), lambda i: (i,), memory_space=pltpu.SMEM),
                  pl.BlockSpec((TOP_K * tc,), lambda i: (jnp.minimum(i + 1, steps - 1),), memory_space=pltpu.SMEM),
                  pl.BlockSpec((tc, ROUTER_LANES), lambda i: (i, 0)), tok,
                  pl.BlockSpec(memory_space=pl.ANY)],
        out_specs=tok,
        scratch_shapes=[slots, slots, pltpu.SemaphoreType.DMA((2,))],
        compiler_params=_cparams("arbitrary"),
        name="combine",
    )(dest, dest, gw, x, ybuf)


def _block_diag(w):
    nb, bd, _ = w.shape
    per = LANES // bd
    w = w.reshape(nb // per, per, bd, bd)
    eye = jnp.eye(per, dtype=w.dtype)
    return (eye[None, :, None, :, None] * w[:, :, :, None, :]).reshape(nb // per, LANES, LANES)


def _moe(x, h2, ids, gw, cnt, xbuf, wg, wu, wd, layer):
    t, d = x.shape
    counts = cnt[EXPERT_LANE0:EXPERT_LANE0 + N_EXPERTS, 0].astype(jnp.int32)
    padded = (counts + MOE_BLOCK - 1) // MOE_BLOCK * MOE_BLOCK
    pend = jnp.cumsum(padded)
    pstart = pend - padded
    lookup = lambda table, idx: jnp.sum(
        jnp.where(idx[..., None] == jnp.arange(table.shape[0], dtype=jnp.int32), table, 0), axis=-1)
    dest = (lookup(pstart, ids[:, 0:TOP_K]) + ids[:, TOP_K:2 * TOP_K]).reshape(t * TOP_K)
    n_blk = xbuf.shape[0] // (MOE_BLOCK * TOKEN_ROWS)
    blk_start = jnp.arange(n_blk, dtype=jnp.int32) * MOE_BLOCK
    blk_expert = jnp.minimum(jnp.sum(pend[None, :] <= blk_start[:, None], axis=1), N_EXPERTS - 1).astype(jnp.int32)
    blocks_used = (pend[-1:] // MOE_BLOCK).astype(jnp.int32)
    xbuf = _dispatch(dest, h2, xbuf)
    ybuf = _experts(blk_expert, blocks_used, xbuf, wg, wu, wd, layer)
    return _combine(dest, gw, x, ybuf), xbuf


def kernel(x, norm_mix, w_in, conv_w, conv_b, lru_w_a, lru_b_a, lru_w_x, lru_b_x, lru_lambda, q_norm, k_norm, norm_out_lru, norm_out_attn, w_out, norm_ffn, router_group_w, router_group_b, router_expert_w, router_expert_b, w_gate, w_up, w_down):
    b, s, d = x.shape
    depth = w_in.shape[0]
    d_attn = norm_out_attn.shape[1]
    pad = ROUTER_LANES - N_GROUPS - N_EXPERTS
    row = lambda v: v.reshape(1, -1)
    t = b * s
    xbuf = jnp.zeros(((t * TOP_K + N_EXPERTS * MOE_BLOCK) * TOKEN_ROWS, LANES), jnp.uint32)
    for l in range(depth):
        y_lru, q, k, v = _in_proj(x, row(norm_mix[l]), w_in[l].astype(BF16), conv_w[l], row(conv_b[l]),
                                  _block_diag(lru_w_a[l]).astype(BF16), row(lru_b_a[l]),
                                  _block_diag(lru_w_x[l]).astype(BF16), row(lru_b_x[l]),
                                  row(lru_lambda[l]), d_attn)
        y_attn = _attention(q, k, v, row(jnp.tile(q_norm[l], HEADS_PER_SLAB)),
                            row(jnp.tile(k_norm[l], HEADS_PER_SLAB)))
        wr = jnp.pad(jnp.concatenate([router_group_w[l], router_expert_w[l]], axis=1), ((0, 0), (0, pad)))
        wr_hi = wr.astype(BF16)
        wr = jnp.concatenate([wr_hi, (wr - wr_hi.astype(F32)).astype(BF16)], axis=1)
        br = jnp.pad(jnp.concatenate([router_group_b[l], router_expert_b[l]]), (0, pad))
        xn, h2, ids, gw, cnt = _out_proj(x, y_lru, y_attn, row(norm_out_lru[l]), row(norm_out_attn[l]),
                                         w_out[l].astype(BF16), row(norm_ffn[l]), wr, row(br))
        x, xbuf = _moe(xn.reshape(t, d), h2, ids.reshape(t, ROUTER_LANES), gw.reshape(t, ROUTER_LANES), cnt,
                       xbuf, w_gate, w_up, w_down, l)
        x = x.reshape(b, s, d)
    return x
```

```python
import functools

import jax
import jax.numpy as jnp
from jax import lax
from jax.experimental import pallas as pl
from jax.experimental.pallas import tpu as pltpu

F32 = jnp.float32
BF16 = jnp.bfloat16

EPS = 1e-6
LANES = 128
HEAD_DIM = 64
HEADS_PER_SLAB = LANES // HEAD_DIM
ATTN_BLOCK = 128
DILATIONS = (1, 4, 16)
CONV_WIDTH = 4
LRU_C = 8.0
N_GROUPS = 4
EXPERTS_PER_GROUP = 8
N_EXPERTS = N_GROUPS * EXPERTS_PER_GROUP
TOP_K = 2
ROUTER_LANES = LANES
EXPERT_LANE0 = N_GROUPS
ROUTER_ROWS = 40
VMEM_LIMIT = 56 * 1024 * 1024

TM_PROJ = 512
SCAN_GROUP = 8
PAD_GROUP = 16
PAD_PITCH = 24
ATTN_UNROLL = 8
NORM_CHUNK = 512
MOE_BLOCK = 1024
TOKEN_ROWS = 4
DMA_TILE = 1024
DISPATCH_TILE = 1024
DMA_UNROLL = 8


def _rms(x, g):
    return x * lax.rsqrt(jnp.mean(x * x, axis=-1, keepdims=True) + EPS) * g


def _log1p(z):
    w = 1.0 + z
    return jnp.where(w == 1.0, z, z * jnp.log(w) / jnp.where(w == 1.0, 1.0, w - 1.0))


def _softplus(x):
    return jnp.maximum(x, 0.0) + _log1p(jnp.exp(-jnp.abs(x)))


def _store_token_tiles(ref, val):
    n, d = val.shape
    assert d == 2 * TOKEN_ROWS * LANES
    for r in range(TOKEN_ROWS):
        hi = val[:, r * LANES:(r + 1) * LANES]
        lo = val[:, d // 2 + r * LANES:d // 2 + (r + 1) * LANES]
        hi = lax.bitcast_convert_type(hi.astype(BF16).astype(F32), jnp.uint32)
        lo = lax.bitcast_convert_type(lo.astype(BF16).astype(F32), jnp.uint32)
        ref[pl.ds(r, n, stride=TOKEN_ROWS), :] = hi | (lo >> 16)


def _load_token_tiles(ref, n):
    words = [ref[pl.ds(r, n, stride=TOKEN_ROWS), :] for r in range(TOKEN_ROWS)]
    hi = [lax.bitcast_convert_type(w & jnp.uint32(0xFFFF0000), F32) for w in words]
    lo = [lax.bitcast_convert_type(w << 16, F32) for w in words]
    return jnp.concatenate(hi + lo, axis=-1)


def _cparams(*sem):
    return pltpu.CompilerParams(dimension_semantics=sem, vmem_limit_bytes=VMEM_LIMIT)


def _in_proj_kernel(x_ref, g_ref, w_ref, cw_ref, cb_ref, wa_ref, ba_ref, wx_ref, bx_ref, lam_ref,
                    y_ref, q_ref, k_ref, v_ref, tail_scr, h_scr, *, d_lru, d_attn):
    h = _rms(x_ref[0], g_ref[...]).astype(BF16)

    def cols(lo, n):
        return jnp.dot(h, w_ref[:, lo:lo + n], preferred_element_type=F32)

    @pl.when(pl.program_id(1) == 0)
    def _():
        tail_scr[...] = jnp.zeros_like(tail_scr)
        h_scr[...] = jnp.zeros_like(h_scr)

    y_ref[0] = _rg_lru_tile(cols(0, d_lru), cols(d_lru, d_lru), cw_ref[...], cb_ref[...], wa_ref[...],
                            ba_ref[...], wx_ref[...], bx_ref[...], lam_ref[...], tail_scr, h_scr)
    base = 2 * d_lru
    for ref in (q_ref, k_ref, v_ref):
        heads = cols(base, d_attn)
        for j in range(d_attn // LANES):
            ref[0, j] = heads[:, j * LANES:(j + 1) * LANES]
        base += d_attn


def _in_proj(x, g, w, cw, cb, wa, ba, wx, bx, lam, d_attn):
    b, s, d = x.shape
    d_lru = cw.shape[1]
    tm = min(TM_PROJ, s)
    n_slab = d_attn // LANES
    slab = jax.ShapeDtypeStruct((b, n_slab, s, LANES), F32)
    slab_spec = pl.BlockSpec((1, n_slab, tm, LANES), lambda i, j: (i, 0, j, 0))
    const = lambda shape: pl.BlockSpec(shape, lambda i, j: (0,) * len(shape))
    vec = const((1, d_lru))
    return pl.pallas_call(
        functools.partial(_in_proj_kernel, d_lru=d_lru, d_attn=d_attn),
        out_shape=(jax.ShapeDtypeStruct((b, s, d_lru), F32), slab, slab, slab),
        grid=(b, s // tm),
        in_specs=[pl.BlockSpec((1, tm, d), lambda i, j: (i, j, 0)), const((1, d)), const(w.shape),
                  const((CONV_WIDTH, d_lru)), vec, const(wa.shape), vec, const(wx.shape), vec, vec],
        out_specs=(pl.BlockSpec((1, tm, d_lru), lambda i, j: (i, j, 0)), slab_spec, slab_spec, slab_spec),
        scratch_shapes=[pltpu.VMEM((8, d_lru), F32), pltpu.VMEM((1, d_lru), F32)],
        compiler_params=_cparams("parallel", "arbitrary"),
        name="in_proj",
    )(x, g, w, cw, cb, wa, ba, wx, bx, lam)


def _rg_lru_tile(x, gate, cw, cb, wa, ba, wx, bx, lam, tail_scr, h_scr):
    tc = x.shape[0]
    halo = tail_scr.shape[0]
    tail = tail_scr[...]
    first = lax.broadcasted_iota(jnp.int32, (halo, 1), 0)
    xc = cb
    for j in range(CONV_WIDTH):
        back = CONV_WIDTH - 1 - j
        if back == 0:
            xs = x
        else:
            shifted = pltpu.roll(x, back, axis=0)
            head = jnp.where(first < back, pltpu.roll(tail, back, axis=0), shifted[0:halo])
            xs = jnp.concatenate([head, shifted[halo:]], axis=0)
        xc = xc + cw[j:j + 1, :] * xs
    tail_scr[...] = x[tc - halo:tc, :]

    xb = xc.astype(BF16)
    r = jax.nn.sigmoid(jnp.dot(xb, wa, preferred_element_type=F32) + ba)
    i = jax.nn.sigmoid(jnp.dot(xb, wx, preferred_element_type=F32) + bx)
    log_a = -LRU_C * r * _softplus(-lam)
    a = jnp.exp(log_a)
    u = jnp.sqrt(jnp.tanh(-log_a) * (1.0 + a * a)) * (i * xc)

    in_group = lax.broadcasted_iota(jnp.int32, (tc, 1), 0) & (SCAN_GROUP - 1)
    shift = 1
    while shift < SCAN_GROUP:
        keep = in_group >= shift
        a_prev = pltpu.roll(a, shift, axis=0)
        u_prev = pltpu.roll(u, shift, axis=0)
        u = jnp.where(keep, a * u_prev + u, u)
        a = jnp.where(keep, a * a_prev, a)
        shift *= 2
    h_last = h_scr[...]
    groups = []
    for g in range(tc // SCAN_GROUP):
        rows = slice(g * SCAN_GROUP, (g + 1) * SCAN_GROUP)
        h_g = a[rows] * h_last + u[rows]
        h_last = h_g[SCAN_GROUP - 1:SCAN_GROUP, :]
        groups.append(h_g)
    h_scr[...] = h_last
    return jnp.concatenate(groups, axis=0) * jax.nn.gelu(gate)


def _attn_kernel(q_ref, k_ref, v_ref, qg_ref, kg_ref, o_ref,
                 qn, kn, m_s, l_s, acc_s, m_d, l_d, acc_d, qp, kp, vp, m_p, l_p, acc_p):
    s_len = q_ref.shape[2]
    blk = ATTN_BLOCK
    lane = lax.broadcasted_iota(jnp.int32, (1, LANES), 1)
    lo = lane < HEAD_DIM
    n_chunk = s_len // NORM_CHUNK
    groups = NORM_CHUNK // PAD_GROUP

    def padded_base(c):
        return pl.multiple_of(c * (groups * PAD_PITCH), 8)

    def to_padded(dst, c, val):
        for g in range(groups):
            dst[pl.ds(padded_base(c) + g * PAD_PITCH, PAD_GROUP), :] = val[g * PAD_GROUP:(g + 1) * PAD_GROUP]

    def from_padded(src, c):
        return jnp.concatenate(
            [src[pl.ds(padded_base(c) + g * PAD_PITCH, PAD_GROUP), :] for g in range(groups)], axis=0)

    same_head = (lax.broadcasted_iota(jnp.int32, (LANES, LANES), 0) // HEAD_DIM
                 == lax.broadcasted_iota(jnp.int32, (LANES, LANES), 1) // HEAD_DIM)
    head_mean = jnp.where(same_head, 1.0 / HEAD_DIM, 0.0).astype(BF16)

    def norm_body(c, carry):
        rows = pl.ds(pl.multiple_of(c * NORM_CHUNK, NORM_CHUNK), NORM_CHUNK)
        for src, g_ref, dst, dst_p, scale in ((q_ref, qg_ref, qn, qp, HEAD_DIM ** -0.5),
                                              (k_ref, kg_ref, kn, kp, 1.0)):
            x = src[0, 0, rows, :]
            x2 = x * x
            x2_hi = x2.astype(BF16)
            x2_lo = (x2 - x2_hi.astype(F32)).astype(BF16)
            ms = (jnp.dot(x2_hi, head_mean, preferred_element_type=F32)
                  + jnp.dot(x2_lo, head_mean, preferred_element_type=F32))
            y = x * lax.rsqrt(ms + EPS) * g_ref[...] * scale
            dst[rows, :] = y
            to_padded(dst_p, c, y)
        to_padded(vp, c, v_ref[0, 0, rows, :])
        return carry

    lax.fori_loop(0, n_chunk, norm_body, 0)

    rows2 = lax.broadcasted_iota(jnp.int32, (HEADS_PER_SLAB * blk, blk), 0)
    qi = rows2 & (blk - 1)
    kj = lax.broadcasted_iota(jnp.int32, (HEADS_PER_SLAB * blk, blk), 1)
    band_prev = kj >= qi
    band_cur = kj <= qi
    nt = (((1,), (1,)), ((), ()))

    def branch(d, q_src, k_src, v_src, stats):
        m_r, l_r, acc_r = stats
        log_d = d.bit_length() - 1
        padded = d == PAD_GROUP
        pitch = PAD_PITCH if padded else d
        span = blk * pitch

        def sl(start):
            return pl.ds(start, blk) if pitch == 1 else pl.ds(start, blk, stride=pitch)

        def body(it, carry):
            n = it >> log_d
            r = it & (d - 1)
            cur = sl(n * span + r)
            prev = sl(jnp.maximum(n - 1, 0) * span + r)
            qt = q_src(cur)
            qs = jnp.concatenate([jnp.where(lo, qt, 0.0), jnp.where(lo, 0.0, qt)], axis=0).astype(BF16)
            s_c = lax.dot_general(qs, k_src(cur).astype(BF16), nt, preferred_element_type=F32)
            s_p = lax.dot_general(qs, k_src(prev).astype(BF16), nt, preferred_element_type=F32)
            s_c = jnp.where(band_cur, s_c, -jnp.inf)
            s_p = jnp.where(band_prev & (n > 0), s_p, -jnp.inf)
            m = jnp.max(jnp.maximum(s_c, s_p), axis=-1, keepdims=True)
            p_c = jnp.exp(s_c - m)
            p_p = jnp.exp(s_p - m)
            l = jnp.sum(p_c + p_p, axis=-1, keepdims=True)
            acc = (jnp.dot(p_c.astype(BF16), v_src(cur).astype(BF16), preferred_element_type=F32)
                   + jnp.dot(p_p.astype(BF16), v_src(prev).astype(BF16), preferred_element_type=F32))
            acc_t = jnp.where(lo, acc[:blk], acc[blk:])
            m_t = jnp.where(lo, jnp.broadcast_to(m[:blk], (blk, LANES)), jnp.broadcast_to(m[blk:], (blk, LANES)))
            l_t = jnp.where(lo, jnp.broadcast_to(l[:blk], (blk, LANES)), jnp.broadcast_to(l[blk:], (blk, LANES)))
            m_r[cur, :] = m_t
            l_r[cur, :] = l_t
            acc_r[cur, :] = acc_t
            return carry

        lax.fori_loop(0, s_len // blk, body, 0, unroll=ATTN_UNROLL)

    token_order = (lambda s: qn[s, :], lambda s: kn[s, :], lambda s: v_ref[0, 0, s, :])
    padded_order = (lambda s: qp[s, :], lambda s: kp[s, :], lambda s: vp[s, :])
    d_near, d_mid, d_far = DILATIONS
    branch(d_near, *token_order, (m_s, l_s, acc_s))
    branch(d_mid, *token_order, (m_d, l_d, acc_d))
    branch(d_far, *padded_order, (m_p, l_p, acc_p))

    def out_body(c, carry):
        rows = pl.ds(pl.multiple_of(c * NORM_CHUNK, NORM_CHUNK), NORM_CHUNK)
        m_a, m_b, m_c = m_s[rows, :], m_d[rows, :], from_padded(m_p, c)
        m_n = jnp.maximum(jnp.maximum(m_a, m_b), m_c)
        c_a = jnp.exp(m_a - m_n)
        c_b = jnp.exp(m_b - m_n)
        c_c = jnp.exp(m_c - m_n)
        o_ref[0, 0, rows, :] = (
            (acc_s[rows, :] * c_a + acc_d[rows, :] * c_b + from_padded(acc_p, c) * c_c)
            / (l_s[rows, :] * c_a + l_d[rows, :] * c_b + from_padded(l_p, c) * c_c))
        return carry

    lax.fori_loop(0, n_chunk, out_body, 0)


def _attention(q, k, v, qg, kg):
    b, n_slab, s, _ = q.shape
    assert s % (ATTN_BLOCK * max(DILATIONS)) == 0 and s % NORM_CHUNK == 0 and max(DILATIONS) == PAD_GROUP
    slab_spec = pl.BlockSpec((1, 1, s, LANES), lambda i, j: (i, j, 0, 0))
    vec = pl.BlockSpec((1, LANES), lambda i, j: (0, 0))
    return pl.pallas_call(
        _attn_kernel,
        out_shape=jax.ShapeDtypeStruct(q.shape, F32),
        grid=(b, n_slab),
        in_specs=[slab_spec, slab_spec, slab_spec, vec, vec],
        out_specs=slab_spec,
        scratch_shapes=([pltpu.VMEM((s, LANES), F32) for _ in range(8)]
                        + [pltpu.VMEM((s // PAD_GROUP * PAD_PITCH, LANES), F32) for _ in range(6)]),
        compiler_params=_cparams("parallel", "parallel"),
        name="dilated_attn",
    )(q, k, v, qg, kg)


def _out_proj_kernel(x_ref, yl_ref, ya_ref, gl_ref, ga_ref, w_ref, gf_ref, wr_ref, br_ref,
                     xn_ref, h2_ref, ids_ref, gw_ref, cnt_ref, carry):
    d_lru = yl_ref.shape[2]
    nl = _rms(yl_ref[0], gl_ref[...]).astype(BF16)
    ya = jnp.concatenate([ya_ref[0, j] for j in range(ya_ref.shape[1])], axis=-1)
    na = _rms(ya, ga_ref[...]).astype(BF16)
    xn = x_ref[0] + (jnp.dot(nl, w_ref[0:d_lru, :], preferred_element_type=F32)
                     + jnp.dot(na, w_ref[d_lru:, :], preferred_element_type=F32))
    xn_ref[0] = xn
    h2 = _rms(xn, gf_ref[...])
    _store_token_tiles(h2_ref, h2)
    hi = h2.astype(BF16)
    lo = (h2 - hi.astype(F32)).astype(BF16)
    both = jnp.dot(hi, wr_ref[...], preferred_element_type=F32)
    logits = (both[:, :ROUTER_LANES] + both[:, ROUTER_LANES:]
              + jnp.dot(lo, wr_ref[:, :ROUTER_LANES], preferred_element_type=F32) + br_ref[...])

    @pl.when((pl.program_id(0) == 0) & (pl.program_id(1) == 0))
    def _():
        carry[...] = jnp.zeros_like(carry)

    ids_ref[0], gw_ref[0] = _route_tile(logits, carry)
    cnt_ref[...] = carry[...]


def _out_proj(x, yl, ya, gl, ga, w, gf, wr, br):
    b, s, d = x.shape
    d_lru = yl.shape[2]
    n_slab = ya.shape[1]
    tm = min(TM_PROJ, s)
    nj = s // tm
    tok = pl.BlockSpec((1, tm, d), lambda i, j: (i, j, 0))
    tiles = pl.BlockSpec((tm * TOKEN_ROWS, LANES), lambda i, j: (i * nj + j, 0))
    lanes = pl.BlockSpec((1, tm, ROUTER_LANES), lambda i, j: (i, j, 0))
    const = lambda shape: pl.BlockSpec(shape, lambda i, j: (0,) * len(shape))
    return pl.pallas_call(
        _out_proj_kernel,
        out_shape=(jax.ShapeDtypeStruct((b, s, d), F32),
                   jax.ShapeDtypeStruct((b * s * TOKEN_ROWS, LANES), jnp.uint32),
                   jax.ShapeDtypeStruct((b, s, ROUTER_LANES), jnp.int32),
                   jax.ShapeDtypeStruct((b, s, ROUTER_LANES), F32),
                   jax.ShapeDtypeStruct((ROUTER_ROWS, LANES), F32)),
        grid=(b, s // tm),
        in_specs=[tok, pl.BlockSpec((1, tm, d_lru), lambda i, j: (i, j, 0)),
                  pl.BlockSpec((1, n_slab, tm, LANES), lambda i, j: (i, 0, j, 0)),
                  const((1, d_lru)), const((1, n_slab * LANES)), const(w.shape), const((1, d)),
                  const(wr.shape), const((1, ROUTER_LANES))],
        out_specs=(tok, tiles, lanes, lanes, const((ROUTER_ROWS, LANES))),
        scratch_shapes=[pltpu.VMEM((ROUTER_ROWS, LANES), F32)],
        compiler_params=_cparams("arbitrary", "arbitrary"),
        name="out_proj",
    )(x, yl, ya, gl, ga, w, gf, wr, br)


def _route_tile(lg, carry):
    tr = lg.shape[0]
    lt = lg.T[:ROUTER_ROWS]
    row = lax.broadcasted_iota(jnp.int32, (ROUTER_ROWS, tr), 0)
    big = jnp.int32(ROUTER_ROWS)

    def argmax(vals):
        top = jnp.max(vals, axis=0, keepdims=True)
        return top, jnp.min(jnp.where(vals == top, row, big), axis=0, keepdims=True)

    g_logit = jnp.where(row < N_GROUPS, lt, -jnp.inf)
    g_top, g_idx = argmax(g_logit)
    p_top = 1.0 / jnp.sum(jnp.exp(g_logit - g_top), axis=0, keepdims=True)
    e_lo = EXPERT_LANE0 + g_idx * EXPERTS_PER_GROUP
    e_logit = jnp.where((row >= e_lo) & (row < e_lo + EXPERTS_PER_GROUP), lt, -jnp.inf)
    v1, i1 = argmax(e_logit)
    v2, i2 = argmax(jnp.where(row == i1, -jnp.inf, e_logit))
    e21 = jnp.exp(v2 - v1)
    w1 = 1.0 / (1.0 + e21) * p_top
    w2 = e21 / (1.0 + e21) * p_top

    hot = ((row == i1) | (row == i2))
    rr = lax.broadcasted_iota(jnp.int32, (tr, tr), 0)
    cc = lax.broadcasted_iota(jnp.int32, (tr, tr), 1)
    earlier = (rr < cc).astype(BF16)
    count = carry[0:ROUTER_ROWS, 0:1]
    prefix = jnp.dot(hot.astype(BF16), earlier, preferred_element_type=F32) + count
    rank1 = jnp.sum(jnp.where(row == i1, prefix, 0.0), axis=0, keepdims=True).astype(jnp.int32)
    rank2 = jnp.sum(jnp.where(row == i2, prefix, 0.0), axis=0, keepdims=True).astype(jnp.int32)
    carry[0:ROUTER_ROWS, :] = carry[0:ROUTER_ROWS, :] + jnp.sum(hot.astype(F32), axis=1, keepdims=True)

    out_row = lax.broadcasted_iota(jnp.int32, (ROUTER_LANES, tr), 0)
    ids_t = jnp.where(out_row == 0, i1 - EXPERT_LANE0,
                      jnp.where(out_row == 1, i2 - EXPERT_LANE0,
                                jnp.where(out_row == 2, rank1, jnp.where(out_row == 3, rank2, 0))))
    gw_t = jnp.where(out_row == 0, w1, jnp.where(out_row == 1, w2, 0.0))
    return ids_t.T, gw_t.T


def _row_copy(src_ref, src_row, dst_ref, dst_row, sem):
    rows = lambda tok: pl.ds(pl.multiple_of(tok * TOKEN_ROWS, TOKEN_ROWS), TOKEN_ROWS)
    return pltpu.make_async_copy(src_ref.at[rows(src_row)], dst_ref.at[rows(dst_row)], sem)


def _dispatch_kernel(dest_ref, h2_ref, xbuf_in_hbm, xbuf_hbm, sem):
    del xbuf_in_hbm
    n_tok = h2_ref.shape[0] // TOKEN_ROWS

    def start(i, carry):
        for k in range(TOP_K):
            _row_copy(h2_ref, i, xbuf_hbm, dest_ref[TOP_K * i + k], sem).start(priority=k)
        return carry

    def wait(i, carry):
        for k in range(TOP_K):
            _row_copy(h2_ref, 0, xbuf_hbm, 0, sem).wait()
        return carry

    lax.fori_loop(0, n_tok, start, 0, unroll=DMA_UNROLL)
    lax.fori_loop(0, n_tok, wait, 0, unroll=DMA_UNROLL)


def _dispatch(dest, h2, xbuf):
    t = h2.shape[0] // TOKEN_ROWS
    td = min(DISPATCH_TILE, t)
    return pl.pallas_call(
        _dispatch_kernel,
        out_shape=jax.ShapeDtypeStruct(xbuf.shape, xbuf.dtype),
        grid=(t // td,),
        in_specs=[pl.BlockSpec((TOP_K * td,), lambda i: (i,), memory_space=pltpu.SMEM),
                  pl.BlockSpec((td * TOKEN_ROWS, LANES), lambda i: (i, 0)),
                  pl.BlockSpec(memory_space=pl.ANY)],
        out_specs=pl.BlockSpec(memory_space=pl.ANY),
        scratch_shapes=[pltpu.SemaphoreType.DMA],
        input_output_aliases={2: 0},
        compiler_params=_cparams("arbitrary"),
        name="dispatch",
    )(dest, h2, xbuf)


def _expert_kernel(be_ref, used_ref, x_ref, wg_ref, wu_ref, wd_ref, y_ref, wg_s, wu_s, wd_s):
    i = pl.program_id(0)
    new_expert = (i == 0) | (be_ref[i] != be_ref[jnp.maximum(i - 1, 0)])
    in_use = i < used_ref[0]

    @pl.when(new_expert & in_use)
    def _():
        wg_s[...] = wg_ref[0, 0].astype(BF16)
        wu_s[...] = wu_ref[0, 0].astype(BF16)
        wd_s[...] = wd_ref[0, 0].astype(BF16)

    @pl.when(in_use)
    def _():
        x = _load_token_tiles(x_ref, MOE_BLOCK).astype(BF16)
        g = jnp.dot(x, wg_s[...], preferred_element_type=F32)
        u = jnp.dot(x, wu_s[...], preferred_element_type=F32)
        hid = (jax.nn.silu(g) * u).astype(BF16)
        _store_token_tiles(y_ref, jnp.dot(hid, wd_s[...], preferred_element_type=F32))

    @pl.when(jnp.logical_not(in_use))
    def _():
        y_ref[...] = jnp.zeros_like(y_ref)


def _experts(blk_expert, blocks_used, xbuf, wg, wu, wd, layer):
    d, de = wg.shape[2], wg.shape[3]
    n_blk = xbuf.shape[0] // (MOE_BLOCK * TOKEN_ROWS)
    rows = pl.BlockSpec((MOE_BLOCK * TOKEN_ROWS, LANES), lambda i, be, used: (i, 0))
    w_in = pl.BlockSpec((1, 1, d, de), lambda i, be, used: (layer, be[i], 0, 0))
    w_out = pl.BlockSpec((1, 1, de, d), lambda i, be, used: (layer, be[i], 0, 0))
    return pl.pallas_call(
        _expert_kernel,
        out_shape=jax.ShapeDtypeStruct(xbuf.shape, xbuf.dtype),
        grid_spec=pltpu.PrefetchScalarGridSpec(
            num_scalar_prefetch=2,
            grid=(n_blk,),
            in_specs=[rows, w_in, w_in, w_out],
            out_specs=rows,
            scratch_shapes=[pltpu.VMEM((d, de), BF16), pltpu.VMEM((d, de), BF16), pltpu.VMEM((de, d), BF16)]),
        compiler_params=_cparams("arbitrary"),
        name="experts",
    )(blk_expert, blocks_used, xbuf, wg, wu, wd)


def _combine_kernel(dest_ref, dest_next_ref, gw_ref, x_ref, ybuf_hbm, o_ref, rows0, rows1, sems):
    n_tok = x_ref.shape[0]
    step = pl.program_id(0)
    slot = step & 1

    def issue(idx_ref, into):
        def start(i, carry):
            for k, rows in enumerate((rows0, rows1)):
                _row_copy(ybuf_hbm, idx_ref[TOP_K * i + k], rows.at[into], i, sems.at[into]).start(priority=k)
            return carry
        lax.fori_loop(0, n_tok, start, 0, unroll=DMA_UNROLL)

    @pl.when(step == 0)
    def _():
        issue(dest_ref, 0)

    @pl.when(step + 1 < pl.num_programs(0))
    def _():
        issue(dest_next_ref, 1 - slot)

    def wait(i, carry):
        for rows in (rows0, rows1):
            _row_copy(ybuf_hbm, 0, rows.at[slot], 0, sems.at[slot]).wait()
        return carry

    lax.fori_loop(0, n_tok, wait, 0, unroll=DMA_UNROLL)
    gw = gw_ref[...]
    o_ref[...] = x_ref[...] + (gw[:, 0:1] * _load_token_tiles(rows0.at[slot], n_tok)
                               + gw[:, 1:2] * _load_token_tiles(rows1.at[slot], n_tok))


def _combine(dest, gw, x, ybuf):
    t, d = x.shape
    tc = min(DMA_TILE, t)
    steps = t // tc
    tok = pl.BlockSpec((tc, d), lambda i: (i, 0))
    slots = pltpu.VMEM((2, tc * TOKEN_ROWS, LANES), ybuf.dtype)
    return pl.pallas_call(
        _combine_kernel,
        out_shape=jax.ShapeDtypeStruct((t, d), F32),
        grid=(steps,),
        in_specs=[pl.BlockSpec((TOP_K * tc,), lambda i: (i,), memory_space=pltpu.SMEM),
                  pl.BlockSpec((TOP_K * tc,), lambda i: (jnp.minimum(i + 1, steps - 1),), memory_space=pltpu.SMEM),
                  pl.BlockSpec((tc, ROUTER_LANES), lambda i: (i, 0)), tok,
                  pl.BlockSpec(memory_space=pl.ANY)],
        out_specs=tok,
        scratch_shapes=[slots, slots, pltpu.SemaphoreType.DMA((2,))],
        compiler_params=_cparams("arbitrary"),
        name="combine",
    )(dest, dest, gw, x, ybuf)


def _block_diag(w):
    nb, bd, _ = w.shape
    eye = jnp.eye(nb, dtype=w.dtype)
    return (eye[:, None, :, None] * w[:, :, None, :]).reshape(nb * bd, nb * bd)


def _moe(x, h2, ids, gw, cnt, xbuf, wg, wu, wd, layer):
    t, d = x.shape
    counts = cnt[EXPERT_LANE0:EXPERT_LANE0 + N_EXPERTS, 0].astype(jnp.int32)
    padded = (counts + MOE_BLOCK - 1) // MOE_BLOCK * MOE_BLOCK
    pend = jnp.cumsum(padded)
    pstart = pend - padded
    lookup = lambda table, idx: jnp.sum(
        jnp.where(idx[..., None] == jnp.arange(table.shape[0], dtype=jnp.int32), table, 0), axis=-1)
    dest = (lookup(pstart, ids[:, 0:TOP_K]) + ids[:, TOP_K:2 * TOP_K]).reshape(t * TOP_K)
    n_blk = xbuf.shape[0] // (MOE_BLOCK * TOKEN_ROWS)
    blk_start = jnp.arange(n_blk, dtype=jnp.int32) * MOE_BLOCK
    blk_expert = jnp.minimum(jnp.sum(pend[None, :] <= blk_start[:, None], axis=1), N_EXPERTS - 1).astype(jnp.int32)
    blocks_used = (pend[-1:] // MOE_BLOCK).astype(jnp.int32)
    xbuf = _dispatch(dest, h2, xbuf)
    ybuf = _experts(blk_expert, blocks_used, xbuf, wg, wu, wd, layer)
    return _combine(dest, gw, x, ybuf), xbuf


def kernel(x, norm_mix, w_in, conv_w, conv_b, lru_w_a, lru_b_a, lru_w_x, lru_b_x, lru_lambda, q_norm, k_norm, norm_out_lru, norm_out_attn, w_out, norm_ffn, router_group_w, router_group_b, router_expert_w, router_expert_b, w_gate, w_up, w_down):
    b, s, d = x.shape
    depth = w_in.shape[0]
    d_attn = norm_out_attn.shape[1]
    pad = ROUTER_LANES - N_GROUPS - N_EXPERTS
    row = lambda v: v.reshape(1, -1)
    t = b * s
    xbuf = jnp.zeros(((t * TOP_K + N_EXPERTS * MOE_BLOCK) * TOKEN_ROWS, LANES), jnp.uint32)
    for l in range(depth):
        y_lru, q, k, v = _in_proj(x, row(norm_mix[l]), w_in[l].astype(BF16), conv_w[l], row(conv_b[l]),
                                  _block_diag(lru_w_a[l]).astype(BF16), row(lru_b_a[l]),
                                  _block_diag(lru_w_x[l]).astype(BF16), row(lru_b_x[l]),
                                  row(lru_lambda[l]), d_attn)
        y_attn = _attention(q, k, v, row(jnp.tile(q_norm[l], HEADS_PER_SLAB)),
                            row(jnp.tile(k_norm[l], HEADS_PER_SLAB)))
        wr = jnp.pad(jnp.concatenate([router_group_w[l], router_expert_w[l]], axis=1), ((0, 0), (0, pad)))
        wr_hi = wr.astype(BF16)
        wr = jnp.concatenate([wr_hi, (wr - wr_hi.astype(F32)).astype(BF16)], axis=1)
        br = jnp.pad(jnp.concatenate([router_group_b[l], router_expert_b[l]]), (0, pad))
        xn, h2, ids, gw, cnt = _out_proj(x, y_lru, y_attn, row(norm_out_lru[l]), row(norm_out_attn[l]),
                                         w_out[l].astype(BF16), row(norm_ffn[l]), wr, row(br))
        x, xbuf = _moe(xn.reshape(t, d), h2, ids.reshape(t, ROUTER_LANES), gw.reshape(t, ROUTER_LANES), cnt,
                       xbuf, w_gate, w_up, w_down, l)
        x = x.reshape(b, s, d)
    return x
```
